```python
import math
import jax, jax.numpy as jnp
from jax import lax
import numpy as np

D_MODEL = 1024
BATCH = 8
SEQ = 8192
DEPTH = 2

ATT_HEADS = 8
ATT_HEAD_DIM = 64
ATT_WIDTH = ATT_HEADS * ATT_HEAD_DIM
POOL_GROUPS = 4
POOL_GROUP_DIM = 64
POOL_WIDTH = POOL_GROUPS * POOL_GROUP_DIM
POOL_WINDOWS = (2, 4, 8, 16)
HGRN_HEADS = 4
HGRN_KEY_DIM = 64
HGRN_VAL_DIM = 64
HGRN_WIDTH = HGRN_HEADS * HGRN_KEY_DIM
MIX_WIDTH = ATT_WIDTH + POOL_WIDTH + HGRN_HEADS * HGRN_VAL_DIM
DILATED_GROUPS = ((128, 1), (512, 4), (2048, 16))
NUM_BUCKETS = 32
MAX_DISTANCE = 1024
HGRN_CHUNK = 64
D_FF = 2816
IN_SPLITS = (ATT_WIDTH, ATT_WIDTH, ATT_WIDTH, POOL_WIDTH,
             HGRN_WIDTH, HGRN_HEADS * HGRN_VAL_DIM, HGRN_WIDTH, HGRN_WIDTH,
             HGRN_HEADS * HGRN_VAL_DIM)
IN_COLS = sum(IN_SPLITS)
EPS = 1e-6
NEG = -1e30

kernel_name = "hybrid_dilated_pool_hgrn2_encoder"


def rmsnorm(x, g):
    xf = x.astype(jnp.float32)
    return xf * lax.rsqrt(jnp.mean(xf * xf, axis=-1, keepdims=True) + EPS) * g.astype(jnp.float32)


def swiglu(h, w_gate, w_up, w_down):
    return (jax.nn.silu(h @ w_gate) * (h @ w_up)) @ w_down


def t5_bucket(rel):
    half = NUM_BUCKETS // 2
    max_exact = half // 2
    base = jnp.where(rel > 0, half, 0)
    n = jnp.abs(rel)
    nf = jnp.maximum(n, 1).astype(jnp.float32)
    large = max_exact + (jnp.log(nf / max_exact) / math.log(MAX_DISTANCE / max_exact)
                         * (half - max_exact)).astype(jnp.int32)
    large = jnp.minimum(large, half - 1)
    return base + jnp.where(n < max_exact, n, large)


def group_bias(table, side, dil):
    qi = jnp.arange(side)[:, None]
    ki = jnp.arange(3 * side)[None, :]
    rel = (ki - side - qi) * dil
    return table[t5_bucket(rel)].astype(jnp.float32).transpose(2, 0, 1)


def dilated_group(q, k, v, bias, side, dil):
    B, H, S, hd = q.shape
    W = side
    L = S // dil
    nb = -(-L // W)
    Lp = nb * W

    def sub(t):
        return t.reshape(B, H, L, dil, hd).transpose(0, 1, 3, 2, 4)

    qb = jnp.pad(sub(q), ((0, 0), (0, 0), (0, 0), (0, Lp - L), (0, 0))).reshape(B, H, dil, nb, W, hd)

    def band(t):
        tp = jnp.pad(sub(t), ((0, 0), (0, 0), (0, 0), (W, Lp - L + W), (0, 0)))
        return jnp.concatenate(
            [tp[:, :, :, j * W:j * W + Lp, :].reshape(B, H, dil, nb, W, hd) for j in range(3)], axis=-2)

    kb, vb = band(k), band(v)
    s = jnp.einsum('bhrnqd,bhrnkd->bhrnqk', qb, kb) * (hd ** -0.5) + bias[:, None, None]
    qi = jnp.arange(W)[:, None]
    ki = jnp.arange(3 * W)[None, :]
    rel = ki - W - qi
    lk = jnp.arange(nb)[:, None, None] * W + ki[None] - W
    valid = (jnp.abs(rel) <= W)[None] & (lk >= 0) & (lk < L)
    s = jnp.where(valid, s, NEG)
    m = jnp.max(s, axis=-1, keepdims=True)
    p = jnp.exp(s - m)
    l = jnp.sum(p, axis=-1)
    o = jnp.einsum('bhrnqk,bhrnkd->bhrnqd', p, vb) / l[..., None]
    o = o.reshape(B, H, dil, Lp, hd)[:, :, :, :L].transpose(0, 1, 3, 2, 4).reshape(B, H, S, hd)
    m = m[..., 0].reshape(B, H, dil, Lp)[..., :L].transpose(0, 1, 3, 2).reshape(B, H, S)
    l = l.reshape(B, H, dil, Lp)[..., :L].transpose(0, 1, 3, 2).reshape(B, H, S)
    return o, m, l


def dilated_attention(zq, zk, zv, q_gain, k_gain, biases):
    B, S, _ = zq.shape
    shp = (B, S, ATT_HEADS, ATT_HEAD_DIM)
    q = rmsnorm(zq.reshape(shp), q_gain).transpose(0, 2, 1, 3)
    k = rmsnorm(zk.reshape(shp), k_gain).transpose(0, 2, 1, 3)
    v = zv.reshape(shp).astype(jnp.float32).transpose(0, 2, 1, 3)
    outs, ms, ls = [], [], []
    for (win, dil), bias in zip(DILATED_GROUPS, biases):
        o, m, l = dilated_group(q, k, v, bias, (win // 2) // dil, dil)
        outs.append(o); ms.append(m); ls.append(l)
    ms = jnp.stack(ms); ls = jnp.stack(ls); outs = jnp.stack(outs)
    wts = ls * jnp.exp(ms - jnp.max(ms, axis=0, keepdims=True))
    o = jnp.sum(wts[..., None] * outs, axis=0) / jnp.sum(wts, axis=0)[..., None]
    return o.transpose(0, 2, 1, 3).reshape(B, S, ATT_WIDTH)


def pool_mixer(u, w_pool, scale):
    B, S, _ = u.shape
    pos = jnp.arange(S)
    outs = []
    for gi, win in enumerate(POOL_WINDOWS):
        ug = u[..., gi * POOL_GROUP_DIM:(gi + 1) * POOL_GROUP_DIM].astype(jnp.float32)
        cs = jnp.concatenate([jnp.zeros((B, 1, POOL_GROUP_DIM), jnp.float32), jnp.cumsum(ug, axis=1)], axis=1)
        lo = jnp.clip(pos - win // 2, 0, S)
        hi = jnp.clip(pos + win - win // 2, 0, S)
        mean = (cs[:, hi] - cs[:, lo]) / (hi - lo).astype(jnp.float32)[None, :, None]
        outs.append(jnp.einsum('bsc,cd->bsd', mean - ug, w_pool[gi]))
    return jnp.concatenate(outs, axis=-1) * scale


def hgrn2_bidirectional(hq, hi, hf_fwd, hf_bwd, hgate, lb_fwd, lb_bwd, out_gain):
    B, S, _ = hq.shape
    C = HGRN_CHUNK
    N = S // C
    f32 = jnp.float32

    def gates(fl, lb):
        fl = fl.astype(f32)
        log_f = jnp.logaddexp(jnp.log(lb), jnp.log1p(-lb) + jax.nn.log_sigmoid(fl))
        k = (1.0 - lb) * jax.nn.sigmoid(-fl)
        return log_f, k

    g_f, k_f = gates(hf_fwd, lb_fwd)
    g_b, k_b = gates(hf_bwd, lb_bwd)
    flip = lambda t: jnp.flip(t, axis=1)
    qf, vf = hq.astype(f32), hi.astype(f32)

    def heads(t, d):
        return t.reshape(2, B, S, HGRN_HEADS, d).transpose(0, 1, 3, 2, 4)

    Q = heads(jnp.stack([qf, flip(qf)]), HGRN_KEY_DIM)
    V = heads(jnp.stack([vf, flip(vf)]), HGRN_VAL_DIM)
    G = heads(jnp.stack([g_f, flip(g_b)]), HGRN_KEY_DIM)
    K = heads(jnp.stack([k_f, flip(k_b)]), HGRN_KEY_DIM)

    def chunks(t):
        return jnp.moveaxis(t.reshape(t.shape[:3] + (N, C, t.shape[-1])), 3, 0)

    tri = jnp.tril(jnp.ones((C, C), dtype=bool))[:, :, None]

    def step(state, inp):
        qc, kc, vc, gc = inp
        b = jnp.cumsum(gc, axis=-2)
        o_inter = jnp.einsum('...tk,...kv->...tv', qc * jnp.exp(b), state)
        diff = b[..., :, None, :] - b[..., None, :, :]
        decay = jnp.exp(jnp.where(tri, diff, -jnp.inf))
        att = jnp.einsum('...tk,...sk,...tsk->...ts', qc, kc, decay)
        o_intra = jnp.einsum('...ts,...sv->...tv', att, vc)
        b_last = b[..., -1:, :]
        new_state = (jnp.swapaxes(jnp.exp(b_last), -1, -2) * state
                     + jnp.einsum('...sk,...sv->...kv', kc * jnp.exp(b_last - b), vc))
        return new_state, o_inter + o_intra

    s0 = jnp.zeros((2, B, HGRN_HEADS, HGRN_KEY_DIM, HGRN_VAL_DIM), f32)
    _, O = lax.scan(step, s0, (chunks(Q), chunks(K), chunks(V), chunks(G)))
    O = jnp.moveaxis(O, 0, 3).reshape(2, B, HGRN_HEADS, S, HGRN_VAL_DIM).transpose(0, 1, 3, 2, 4)
    o = O[0] + flip(O[1])
    o = rmsnorm(o, out_gain) * jax.nn.silu(hgate.astype(f32).reshape(B, S, HGRN_HEADS, HGRN_VAL_DIM))
    return o.reshape(B, S, HGRN_HEADS * HGRN_VAL_DIM)


def setup_inputs(seed: int = 0) -> dict:
    key = jax.random.key(seed)
    ks = jax.random.split(key, 20)
    nrm = lambda k, shape, sc: jax.random.normal(k, shape, jnp.float32) * sc
    return {
        "x": nrm(ks[0], (BATCH, SEQ, D_MODEL), 1.0),
        "ffn1_norm": 1.0 + nrm(ks[1], (DEPTH, D_MODEL), 0.1),
        "ffn1_w_gate": nrm(ks[2], (DEPTH, D_MODEL, D_FF), D_MODEL ** -0.5),
        "ffn1_w_up": nrm(ks[3], (DEPTH, D_MODEL, D_FF), D_MODEL ** -0.5),
        "ffn1_w_down": nrm(ks[4], (DEPTH, D_FF, D_MODEL), D_FF ** -0.5),
        "mix_norm": 1.0 + nrm(ks[5], (DEPTH, D_MODEL), 0.1),
        "w_in": nrm(ks[6], (DEPTH, D_MODEL, IN_COLS), D_MODEL ** -0.5),
        "q_norm": 1.0 + nrm(ks[7], (DEPTH, ATT_HEADS, ATT_HEAD_DIM), 0.1),
        "k_norm": 1.0 + nrm(ks[8], (DEPTH, ATT_HEADS, ATT_HEAD_DIM), 0.1),
        "rel_bias": nrm(ks[9], (NUM_BUCKETS, ATT_HEADS), 0.5),
        "pool_w": nrm(ks[10], (DEPTH, POOL_GROUPS, POOL_GROUP_DIM, POOL_GROUP_DIM), POOL_GROUP_DIM ** -0.5),
        "pool_scale": 1.0 + nrm(ks[11], (DEPTH, POOL_WIDTH), 0.1),
        "hgrn_lb_logits": nrm(ks[12], (2, DEPTH, HGRN_WIDTH), 1.0),
        "hgrn_norm": 1.0 + nrm(ks[13], (DEPTH, HGRN_VAL_DIM), 0.1),
        "w_out": nrm(ks[14], (DEPTH, MIX_WIDTH, D_MODEL), MIX_WIDTH ** -0.5),
        "ffn2_norm": 1.0 + nrm(ks[15], (DEPTH, D_MODEL), 0.1),
        "ffn2_w_gate": nrm(ks[16], (DEPTH, D_MODEL, D_FF), D_MODEL ** -0.5),
        "ffn2_w_up": nrm(ks[17], (DEPTH, D_MODEL, D_FF), D_MODEL ** -0.5),
        "ffn2_w_down": nrm(ks[18], (DEPTH, D_FF, D_MODEL), D_FF ** -0.5),
    }


def reference(x, ffn1_norm, ffn1_w_gate, ffn1_w_up, ffn1_w_down, mix_norm, w_in,
              q_norm, k_norm, rel_bias, pool_w, pool_scale, hgrn_lb_logits, hgrn_norm,
              w_out, ffn2_norm, ffn2_w_gate, ffn2_w_up, ffn2_w_down):
    in_dtype = x.dtype
    h_res = x.astype(jnp.float32)
    biases = [group_bias(rel_bias, (win // 2) // dil, dil) for win, dil in DILATED_GROUPS]
    lb_cum = jnp.cumsum(jax.nn.softmax(hgrn_lb_logits.astype(jnp.float32), axis=1), axis=1)
    lb_all = lb_cum - lb_cum[:, :1]
    split_idx = [int(v) for v in np.cumsum(IN_SPLITS)[:-1]]
    for l in range(DEPTH):
        h_res = h_res + 0.5 * swiglu(rmsnorm(h_res, ffn1_norm[l]), ffn1_w_gate[l], ffn1_w_up[l], ffn1_w_down[l])
        z = rmsnorm(h_res, mix_norm[l]) @ w_in[l]
        zq, zk, zv, zp, hq, hi, hff, hfb, hg = jnp.split(z, split_idx, axis=-1)
        y_att = dilated_attention(zq, zk, zv, q_norm[l], k_norm[l], biases)
        y_pool = pool_mixer(zp, pool_w[l], pool_scale[l])
        y_rec = hgrn2_bidirectional(hq, hi, hff, hfb, hg, lb_all[0, l], lb_all[1, l], hgrn_norm[l])
        h_res = h_res + jnp.concatenate([y_att, y_pool, y_rec], axis=-1) @ w_out[l]
        h_res = h_res + 0.5 * swiglu(rmsnorm(h_res, ffn2_norm[l]), ffn2_w_gate[l], ffn2_w_up[l], ffn2_w_down[l])
    return h_res.astype(in_dtype)
```

```python
import functools
import math

import numpy as np
import jax
import jax.numpy as jnp
from jax import lax
from jax.experimental import pallas as pl
from jax.experimental.pallas import tpu as pltpu

F32 = jnp.float32
BF16 = jnp.bfloat16

LANES = 128
SUBLANES = 8
VMEM_BYTES_V7X = 64 * 1024 * 1024
VMEM_LIMIT = VMEM_BYTES_V7X - 8 * 1024 * 1024

HEAD_DIM = 64
HEADS_PER_TILE = LANES // HEAD_DIM
ATT_WIDTH = 512
POOL_WIDTH = 256
HGRN_WIDTH = 256
POOL_WINDOWS = (2, 4, 8, 16)
DILATIONS = (1, 4, 16)
ATT_SIDE = 64
NUM_BUCKETS = 32
MAX_DISTANCE = 1024
EPS = 1e-6
NEG = -1e30

Q_BLOCK = 128
K_BLOCK = Q_BLOCK + 2 * ATT_SIDE
HGRN_CHUNK = 64
HGRN_SUB = SUBLANES
HGRN_LEVELS = 3


def _cparams(sem):
    return pltpu.CompilerParams(dimension_semantics=sem, vmem_limit_bytes=VMEM_LIMIT)


def _resident(shape):
    nd = len(shape)
    return pl.BlockSpec(shape, lambda *_: (0,) * nd, pipeline_mode=pl.Buffered(1))


def _head_blockdiag():
    r = np.arange(LANES) // HEAD_DIM
    return jnp.asarray(r[:, None] == r[None, :], dtype=BF16)


def _head_mean_sq(v, bd):
    return jnp.dot((v * v).astype(BF16), bd, preferred_element_type=F32) * (1.0 / HEAD_DIM)


def _ffn_body(x_ref, g_ref, wg_ref, wu_ref, wd_ref, o_ref, a_ref, *, ff_chunk):
    x = x_ref[...]
    h = (x * lax.rsqrt(jnp.mean(x * x, axis=-1, keepdims=True) + EPS) * g_ref[...]).astype(BF16)
    for c in range(wg_ref.shape[1] // ff_chunk):
        sl = slice(c * ff_chunk, (c + 1) * ff_chunk)
        gate = jnp.dot(h, wg_ref[:, sl], preferred_element_type=F32)
        up = jnp.dot(h, wu_ref[:, sl], preferred_element_type=F32)
        a_ref[:, sl] = (gate * jax.nn.sigmoid(gate) * up).astype(BF16)
    o_ref[...] = x + 0.5 * jnp.dot(a_ref[...], wd_ref[...], preferred_element_type=F32)


def ffn(x2, gain, wg, wu, wd, *, tm=512, ff_chunk=256):
    n, d = x2.shape
    dff = wg.shape[1]
    return pl.pallas_call(
        functools.partial(_ffn_body, ff_chunk=ff_chunk),
        grid=(n // tm,),
        in_specs=[pl.BlockSpec((tm, d), lambda i: (i, 0)),
                  _resident((1, d)), _resident((d, dff)), _resident((d, dff)), _resident((dff, d))],
        out_specs=pl.BlockSpec((tm, d), lambda i: (i, 0)),
        out_shape=jax.ShapeDtypeStruct((n, d), F32),
        scratch_shapes=[pltpu.VMEM((tm, dff), BF16)],
        compiler_params=_cparams(("parallel",)),
    )(x2, gain.reshape(1, d), wg, wu, wd)


def _inproj_body(x_ref, g_ref, w_ref, qg_ref, kg_ref, bd_ref, qkv_ref, rest_ref):
    x = x_ref[...]
    h = (x * lax.rsqrt(jnp.mean(x * x, axis=-1, keepdims=True) + EPS) * g_ref[...]).astype(BF16)
    bd = bd_ref[...]
    for c in range(2 * ATT_WIDTH // LANES):
        sl = slice(c * LANES, (c + 1) * LANES)
        z = jnp.dot(h, w_ref[:, sl], preferred_element_type=F32)
        if c < ATT_WIDTH // LANES:
            gain = qg_ref[:, sl] * (HEAD_DIM ** -0.5)
        else:
            ck = c - ATT_WIDTH // LANES
            gain = kg_ref[:, ck * LANES:(ck + 1) * LANES]
        qkv_ref[:, sl] = z * lax.rsqrt(_head_mean_sq(z, bd) + EPS) * gain
    v0 = 2 * ATT_WIDTH
    for c in range(ATT_WIDTH // 256):
        sl = slice(v0 + c * 256, v0 + (c + 1) * 256)
        qkv_ref[:, sl] = jnp.dot(h, w_ref[:, sl], preferred_element_type=F32)
    r0 = 3 * ATT_WIDTH
    for c in range(rest_ref.shape[1] // 256):
        rest_ref[:, c * 256:(c + 1) * 256] = jnp.dot(
            h, w_ref[:, r0 + c * 256:r0 + (c + 1) * 256], preferred_element_type=F32)


def inproj(x2, gain, w_in, q_gain, k_gain, *, tm=512):
    n, d = x2.shape
    cols = w_in.shape[1]
    n_att = 3 * ATT_WIDTH
    return pl.pallas_call(
        _inproj_body,
        grid=(n // tm,),
        in_specs=[pl.BlockSpec((tm, d), lambda i: (i, 0)),
                  _resident((1, d)), _resident((d, cols)),
                  _resident((1, ATT_WIDTH)), _resident((1, ATT_WIDTH)), _resident((LANES, LANES))],
        out_specs=[pl.BlockSpec((tm, n_att), lambda i: (i, 0)),
                   pl.BlockSpec((tm, cols - n_att), lambda i: (i, 0))],
        out_shape=[jax.ShapeDtypeStruct((n, n_att), F32),
                   jax.ShapeDtypeStruct((n, cols - n_att), F32)],
        compiler_params=_cparams(("parallel",)),
    )(x2, gain.reshape(1, d), w_in, q_gain.reshape(1, ATT_WIDTH), k_gain.reshape(1, ATT_WIDTH),
      _head_blockdiag())


def _t5_bucket(rel):
    half = NUM_BUCKETS // 2
    max_exact = half // 2
    base = jnp.where(rel > 0, half, 0)
    n = jnp.abs(rel)
    nf = jnp.maximum(n, 1).astype(F32)
    large = max_exact + (jnp.log(nf / max_exact) / math.log(MAX_DISTANCE / max_exact)
                         * (half - max_exact)).astype(jnp.int32)
    large = jnp.minimum(large, half - 1)
    return base + jnp.where(n < max_exact, n, large)


def _bias_buckets():
    qi = jnp.arange(Q_BLOCK)[:, None]
    ki = jnp.arange(K_BLOCK)[None, :]
    out = []
    for dil in DILATIONS:
        per_var = []
        for var in range(3):
            rel = ki - var * ATT_SIDE - qi
            per_var.append(jnp.where(jnp.abs(rel) <= ATT_SIDE, _t5_bucket(rel * dil), -1))
        out.append(jnp.stack(per_var))
    return jnp.stack(out).astype(jnp.int32)


def _bias_body(tbl_ref, bk_ref, o_ref):
    hp = pl.program_id(0)
    for var in range(3):
        bk = bk_ref[var]
        for h in range(HEADS_PER_TILE):
            head = hp * HEADS_PER_TILE + h
            acc = jnp.full(bk.shape, NEG, F32)
            for b in range(NUM_BUCKETS):
                acc = jnp.where(bk == b, tbl_ref[b, head], acc)
            o_ref[var, h] = acc


def attention_bias(rel_bias):
    n_pairs = rel_bias.shape[1] // HEADS_PER_TILE
    n_g = len(DILATIONS)
    return pl.pallas_call(
        _bias_body,
        grid=(n_pairs, n_g),
        in_specs=[pl.BlockSpec(memory_space=pltpu.SMEM),
                  pl.BlockSpec((None, 3, Q_BLOCK, K_BLOCK), lambda p, g: (g, 0, 0, 0))],
        out_specs=pl.BlockSpec((None, None, 3, HEADS_PER_TILE, Q_BLOCK, K_BLOCK),
                               lambda p, g: (p, g, 0, 0, 0, 0)),
        out_shape=jax.ShapeDtypeStruct((n_pairs, n_g, 3, HEADS_PER_TILE, Q_BLOCK, K_BLOCK), F32),
        compiler_params=_cparams(("parallel", "parallel")),
    )(rel_bias.astype(F32), _bias_buckets())


def _attn_body(q_ref, k_ref, v_ref, bias_ref, o_ref, m_ref, l_ref):
    seq = q_ref.shape[0]
    lane = lax.broadcasted_iota(jnp.int32, (Q_BLOCK, LANES), 1)
    lo = lane < HEAD_DIM
    ones = jnp.ones((K_BLOCK, LANES), BF16)

    def rows(start, n, dil):
        return pl.ds(start, n) if dil == 1 else pl.ds(start, n, stride=dil)

    def block(g, dil, qs, ks, var):
        qrows, krows = rows(qs, Q_BLOCK, dil), rows(ks, K_BLOCK, dil)
        q = q_ref[qrows, :]
        k = k_ref[krows, :].astype(BF16)
        v = v_ref[krows, :].astype(BF16)
        per_head = []
        for h in range(HEADS_PER_TILE):
            qh = jnp.where(lo if h == 0 else jnp.logical_not(lo), q, 0.0).astype(BF16)
            s = lax.dot_general(qh, k, (((1,), (1,)), ((), ())), preferred_element_type=F32)
            s = s + bias_ref[g, var, h]
            mh = jnp.max(s, axis=1, keepdims=True)
            p = jnp.exp(s - mh).astype(BF16)
            per_head.append((jnp.dot(p, v, preferred_element_type=F32),
                             jnp.dot(p, ones, preferred_element_type=F32),
                             jnp.broadcast_to(mh, (Q_BLOCK, LANES))))
        o, l, m = (jnp.where(lo, a, b) for a, b in zip(*per_head))
        if g > 0:
            m_old = m_ref[qrows, :]
            m_new = jnp.maximum(m_old, m)
            a_old, a_cur = jnp.exp(m_old - m_new), jnp.exp(m - m_new)
            o = o_ref[qrows, :] * a_old + o * a_cur
            l = l_ref[qrows, :] * a_old + l * a_cur
            m = m_new
        o_ref[qrows, :] = o
        l_ref[qrows, :] = l
        m_ref[qrows, :] = m

    for g, dil in enumerate(DILATIONS):
        sub_len = seq // dil
        n_blocks = sub_len // Q_BLOCK

        def residue(r, carry, g=g, dil=dil, sub_len=sub_len, n_blocks=n_blocks):
            block(g, dil, r, r, 0)

            def interior(n, c):
                qs = n * Q_BLOCK
                if dil == 1:
                    qs = pl.multiple_of(qs, Q_BLOCK)
                block(g, dil, r + dil * qs, r + dil * (qs - ATT_SIDE), 1)
                return c

            lax.fori_loop(1, n_blocks - 1, interior, 0)
            block(g, dil, r + dil * (sub_len - Q_BLOCK), r + dil * (sub_len - K_BLOCK), 2)
            return carry

        if dil == 1:
            residue(0, 0)
        else:
            lax.fori_loop(0, dil, residue, 0)

    def normalise(i, c):
        sl = pl.ds(pl.multiple_of(i * Q_BLOCK, Q_BLOCK), Q_BLOCK)
        o_ref[sl, :] = o_ref[sl, :] / l_ref[sl, :]
        return c

    lax.fori_loop(0, seq // Q_BLOCK, normalise, 0)


def attention(qkv, bias):
    bsz, seq, _ = qkv.shape
    n_pairs = ATT_WIDTH // LANES
    assert seq % (max(DILATIONS) * Q_BLOCK) == 0 and seq // max(DILATIONS) >= K_BLOCK

    def col(c0):
        return pl.BlockSpec((None, seq, LANES), lambda b, p: (b, 0, c0 + p))

    return pl.pallas_call(
        _attn_body,
        grid=(bsz, n_pairs),
        in_specs=[col(0), col(n_pairs), col(2 * n_pairs),
                  pl.BlockSpec((None,) + bias.shape[1:], lambda b, p: (p, 0, 0, 0, 0, 0))],
        out_specs=pl.BlockSpec((None, seq, LANES), lambda b, p: (b, 0, p)),
        out_shape=jax.ShapeDtypeStruct((bsz, seq, ATT_WIDTH), F32),
        scratch_shapes=[pltpu.VMEM((seq, LANES), F32), pltpu.VMEM((seq, LANES), F32)],
        compiler_params=_cparams(("parallel", "parallel")),
    )(qkv, qkv, qkv, bias)


POOL_HALO = max(POOL_WINDOWS) // 2


def _pool_body(u_ref, prev_ref, next_ref, w_ref, scale_ref, o_ref, pad_ref, *, seq):
    t = pl.program_id(1)
    ts = u_ref.shape[0]
    u = u_ref[...]
    pad_ref[pl.ds(0, POOL_HALO), :] = jnp.where(t > 0, prev_ref[...], 0.0)
    pad_ref[pl.ds(POOL_HALO, ts), :] = u
    pad_ref[pl.ds(POOL_HALO + ts, POOL_HALO), :] = jnp.where(t < pl.num_programs(1) - 1, next_ref[...], 0.0)

    def shifted(off):
        return pad_ref[pl.ds(POOL_HALO + off, ts), :]

    lane_win = lax.broadcasted_iota(jnp.int32, (1, POOL_WIDTH), 1) // HEAD_DIM
    pos = t * ts + lax.broadcasted_iota(jnp.int32, (ts, 1), 0)
    total = jnp.zeros_like(u)
    cnt = jnp.zeros_like(u)
    acc = None
    lo_off, hi_off = 0, 0
    for gi, win in enumerate(POOL_WINDOWS):
        for off in list(range(-(win // 2), lo_off)) + list(range(hi_off, win - win // 2)):
            acc = shifted(off) if acc is None else acc + shifted(off)
        lo_off, hi_off = -(win // 2), win - win // 2
        in_group = lane_win == gi
        n = (jnp.minimum(pos + hi_off, seq) - jnp.maximum(pos + lo_off, 0)).astype(F32)
        total = jnp.where(in_group, acc, total)
        cnt = jnp.where(in_group, n, cnt)
    mixed = jnp.dot((total / cnt - u).astype(BF16), w_ref[...], preferred_element_type=F32)
    o_ref[...] = mixed * scale_ref[...]


def pool_mixer(rest, w_blockdiag, scale, *, ts=1024):
    bsz, seq, _ = rest.shape
    ts = min(ts, seq)
    per_tile = ts // POOL_HALO
    last_halo = seq // POOL_HALO - 1
    return pl.pallas_call(
        functools.partial(_pool_body, seq=seq),
        grid=(bsz, seq // ts),
        in_specs=[pl.BlockSpec((None, ts, POOL_WIDTH), lambda b, t: (b, t, 0)),
                  pl.BlockSpec((None, POOL_HALO, POOL_WIDTH),
                               lambda b, t: (b, jnp.maximum(t * per_tile - 1, 0), 0)),
                  pl.BlockSpec((None, POOL_HALO, POOL_WIDTH),
                               lambda b, t: (b, jnp.minimum((t + 1) * per_tile, last_halo), 0)),
                  _resident((POOL_WIDTH, POOL_WIDTH)), _resident((1, POOL_WIDTH))],
        out_specs=pl.BlockSpec((None, ts, POOL_WIDTH), lambda b, t: (b, t, 0)),
        out_shape=jax.ShapeDtypeStruct((bsz, seq, POOL_WIDTH), F32),
        scratch_shapes=[pltpu.VMEM((ts + 2 * POOL_HALO, POOL_WIDTH), F32)],
        compiler_params=_cparams(("parallel", "parallel")),
    )(rest, rest, rest, w_blockdiag, scale.reshape(1, POOL_WIDTH))


def _pool_blockdiag(w_pool):
    n_g = w_pool.shape[0]
    eye = jnp.eye(n_g, dtype=w_pool.dtype)
    return jnp.einsum('gcd,gh->gchd', w_pool, eye).reshape(n_g * HEAD_DIM, n_g * HEAD_DIM)


def _hgrn_constants():
    c = HGRN_CHUNK
    t = np.arange(c)
    s_col = np.tile(t, HEADS_PER_TILE)
    cum, masks = [], []
    for direction in range(2):
        tri = (t[None, :] <= t[:, None]) if direction == 0 else (t[None, :] >= t[:, None])
        tri = tri.astype(np.float32)
        mats, qms, bms = [tri], [], []
        for level in range(HGRN_LEVELS):
            half = HGRN_SUB << level
            start = (t // (2 * half)) * (2 * half)
            boundary = start + (half - 1 if direction == 0 else half)
            mats.append(tri[boundary])
            q_right = (t // half) % 2 == 1
            qm = q_right if direction == 0 else ~q_right
            qms.append(np.broadcast_to(qm[:, None], (c, LANES)))
            bms.append(t[:, None] // (2 * half) == s_col[None, :] // (2 * half))
        same_sub = t[:, None] // HGRN_SUB == s_col[None, :] // HGRN_SUB
        causal = (s_col[None, :] <= t[:, None]) if direction == 0 else (s_col[None, :] >= t[:, None])
        cum.append(np.concatenate(mats, axis=0))
        masks.append(np.stack(qms + bms[:HGRN_LEVELS - 1] + [same_sub & causal]).astype(np.float32))
    lane_head = np.arange(LANES) // HEAD_DIM
    sel = np.stack([(lane_head[:, None] == lane_head[None, :]) & (s_col[None, :] % HGRN_SUB == j)
                    for j in range(HGRN_SUB)]).astype(np.float32)
    return (jnp.asarray(np.stack(cum), BF16), jnp.asarray(np.stack(masks), F32), jnp.asarray(sel, BF16))


def _split3(x):
    x1 = x.astype(BF16)
    r1 = x - x1.astype(F32)
    x2 = r1.astype(BF16)
    x3 = (r1 - x2.astype(F32)).astype(BF16)
    return x1, x2, x3


def _nt(a, b):
    return lax.dot_general(a, b, (((1,), (1,)), ((), ())), preferred_element_type=F32)


def _tn(a, b):
    return lax.dot_general(a, b, (((0,), (0,)), ((), ())), preferred_element_type=F32)


def _hgrn_chunk(direction, row0, q_ref, v_ref, f_ref, lb_ref, cum_ref, mask_ref, sel_ref,
                o_ref, state_ref, b_scr, k_scr):
    c = HGRN_CHUNK
    rows = pl.ds(pl.multiple_of(row0, c), c)
    q, v, fl = q_ref[rows, :], v_ref[rows, :], f_ref[rows, :]
    lb = lb_ref[...]
    lane = lax.broadcasted_iota(jnp.int32, (c, LANES), 1)
    lo = lane < HEAD_DIM

    e = jnp.exp(-jnp.abs(fl))
    log_sig = jnp.minimum(fl, 0.0) - jnp.log1p(e)
    rcp = 1.0 / (1.0 + e)
    kk = (1.0 - lb) * jnp.where(fl >= 0, e * rcp, rcp)
    log_lb = jnp.log(lb)
    cc = jnp.log1p(-lb) + log_sig
    g = jnp.maximum(log_lb, cc) + jnp.log1p(jnp.exp(-jnp.abs(log_lb - cc)))

    cum = cum_ref[direction]
    ball = sum(jnp.dot(cum, part, preferred_element_type=F32) for part in _split3(g))
    b = ball[0:c]
    b_edge = b[c - 1:c] if direction == 0 else b[0:1]

    def by_head(x):
        return jnp.concatenate([jnp.where(lo, x, 0.0), jnp.where(lo, 0.0, x)], axis=0).astype(BF16)

    state = state_ref[...]
    o = _nt((q * jnp.exp(b)).astype(BF16), state.astype(BF16))
    k_out = kk * jnp.exp(b_edge - b)
    upd = _tn(v.astype(BF16), k_out.astype(BF16))
    r128 = lax.broadcasted_iota(jnp.int32, (LANES, LANES), 0) // HEAD_DIM
    c128 = lax.broadcasted_iota(jnp.int32, (LANES, LANES), 1) // HEAD_DIM
    state_ref[...] = jnp.where(r128 == c128, state * jnp.exp(b_edge) + upd, 0.0)

    att = None
    for level in range(HGRN_LEVELS):
        beta = ball[(level + 1) * c:(level + 2) * c]
        qm = mask_ref[direction, level]
        ql = q * jnp.exp(jnp.minimum(b - beta, 0.0)) * qm
        kl = kk * jnp.exp(jnp.minimum(beta - b, 0.0)) * (1.0 - qm)
        a = _nt(ql.astype(BF16), by_head(kl))
        if level < HGRN_LEVELS - 1:
            a = a * mask_ref[direction, HGRN_LEVELS + level]
        att = a if att is None else att + a

    b_scr[direction] = b
    k_scr[direction] = kk
    n_sub = c // HGRN_SUB
    diag = None
    for j in range(HGRN_SUB):
        pieces = []
        for u in range(n_sub):
            r = u * HGRN_SUB
            b_row = b_scr[direction, pl.ds(r + j, 1), :]
            k_row = k_scr[direction, pl.ds(r + j, 1), :]
            pieces.append((q[r:r + HGRN_SUB] * k_row)
                          * jnp.exp(jnp.minimum(b[r:r + HGRN_SUB] - b_row, 0.0)))
        term = jnp.dot(jnp.concatenate(pieces, axis=0).astype(BF16), sel_ref[j],
                       preferred_element_type=F32)
        diag = term if diag is None else diag + term
    att = att + diag * mask_ref[direction, 2 * HGRN_LEVELS - 1]

    o_ref[rows, :] = o + jnp.dot(att.astype(BF16), by_head(v), preferred_element_type=F32)


def _hgrn_body(qf_ref, vf_ref, ff_ref, qb_ref, vb_ref, fb_ref, lbf_ref, lbb_ref,
               cum_ref, mask_ref, sel_ref, of_ref, ob_ref, sf_ref, sb_ref, b_scr, k_scr):
    @pl.when(pl.program_id(2) == 0)
    def _():
        sf_ref[...] = jnp.zeros_like(sf_ref)
        sb_ref[...] = jnp.zeros_like(sb_ref)

    n_chunks = qf_ref.shape[0] // HGRN_CHUNK

    def step(i, carry):
        _hgrn_chunk(0, i * HGRN_CHUNK, qf_ref, vf_ref, ff_ref, lbf_ref, cum_ref, mask_ref, sel_ref,
                    of_ref, sf_ref, b_scr, k_scr)
        _hgrn_chunk(1, (n_chunks - 1 - i) * HGRN_CHUNK, qb_ref, vb_ref, fb_ref, lbb_ref, cum_ref,
                    mask_ref, sel_ref, ob_ref, sb_ref, b_scr, k_scr)
        return carry

    lax.fori_loop(0, n_chunks, step, 0)


def hgrn_scan(rest, lb_fwd, lb_bwd, *, ts=1024):
    bsz, seq, _ = rest.shape
    ts = min(ts, seq)
    nt = seq // ts
    n_lt = HGRN_WIDTH // LANES
    cum, masks, sel = _hgrn_constants()

    def fwd(c0):
        return pl.BlockSpec((None, ts, LANES), lambda b, t, j: (b, j, c0 + t))

    def bwd(c0):
        return pl.BlockSpec((None, ts, LANES), lambda b, t, j: (b, nt - 1 - j, c0 + t))

    lb_spec = pl.BlockSpec((1, LANES), lambda b, t, j: (0, t))
    return pl.pallas_call(
        _hgrn_body,
        grid=(bsz, n_lt, nt),
        in_specs=[fwd(2), fwd(4), fwd(6), bwd(2), bwd(4), bwd(8), lb_spec, lb_spec,
                  _resident(cum.shape), _resident(masks.shape), _resident(sel.shape)],
        out_specs=[pl.BlockSpec((None, ts, LANES), lambda b, t, j: (b, j, t)),
                   pl.BlockSpec((None, ts, LANES), lambda b, t, j: (b, nt - 1 - j, t))],
        out_shape=[jax.ShapeDtypeStruct((bsz, seq, HGRN_WIDTH), F32)] * 2,
        scratch_shapes=[pltpu.VMEM((LANES, LANES), F32), pltpu.VMEM((LANES, LANES), F32),
                        pltpu.VMEM((2, HGRN_CHUNK, LANES), F32), pltpu.VMEM((2, HGRN_CHUNK, LANES), F32)],
        compiler_params=_cparams(("parallel", "parallel", "arbitrary")),
    )(rest, rest, rest, rest, rest, rest, lb_fwd.reshape(1, HGRN_WIDTH), lb_bwd.reshape(1, HGRN_WIDTH),
      cum, masks, sel)


def _outproj_body(x_ref, att_ref, pool_ref, of_ref, ob_ref, gate_ref, gain_ref, bd_ref, w_ref, o_ref):
    bd = bd_ref[...]
    acc = x_ref[...]
    acc = acc + jnp.dot(att_ref[...].astype(BF16), w_ref[0:ATT_WIDTH, :], preferred_element_type=F32)
    acc = acc + jnp.dot(pool_ref[...].astype(BF16), w_ref[ATT_WIDTH:ATT_WIDTH + POOL_WIDTH, :],
                        preferred_element_type=F32)
    gate = gate_ref[...]
    rec = []
    for c in range(HGRN_WIDTH // LANES):
        sl = slice(c * LANES, (c + 1) * LANES)
        o = of_ref[:, sl] + ob_ref[:, sl]
        y = o * lax.rsqrt(_head_mean_sq(o, bd) + EPS) * gain_ref[:, sl]
        gt = gate[:, sl]
        rec.append((y * (gt * jax.nn.sigmoid(gt))).astype(BF16))
    acc = acc + jnp.dot(jnp.concatenate(rec, axis=1), w_ref[ATT_WIDTH + POOL_WIDTH:, :],
                        preferred_element_type=F32)
    o_ref[...] = acc


def outproj(x2, y_att, y_pool, o_fwd, o_bwd, rest, out_gain, w_out, *, tm=512):
    n, d = x2.shape
    gate_tile = (rest.shape[1] - HGRN_WIDTH) // HGRN_WIDTH

    def rowblock(width, col=0):
        return pl.BlockSpec((tm, width), lambda i: (i, col))

    return pl.pallas_call(
        _outproj_body,
        grid=(n // tm,),
        in_specs=[rowblock(d), rowblock(ATT_WIDTH), rowblock(POOL_WIDTH), rowblock(HGRN_WIDTH),
                  rowblock(HGRN_WIDTH), rowblock(HGRN_WIDTH, gate_tile),
                  _resident((1, HGRN_WIDTH)), _resident((LANES, LANES)), _resident(w_out.shape)],
        out_specs=rowblock(d),
        out_shape=jax.ShapeDtypeStruct((n, d), F32),
        compiler_params=_cparams(("parallel",)),
    )(x2, y_att, y_pool, o_fwd, o_bwd, rest,
      jnp.tile(out_gain, HGRN_WIDTH // HEAD_DIM).reshape(1, HGRN_WIDTH), _head_blockdiag(), w_out)


def kernel(x, ffn1_norm, ffn1_w_gate, ffn1_w_up, ffn1_w_down, mix_norm, w_in, q_norm, k_norm, rel_bias,
           pool_w, pool_scale, hgrn_lb_logits, hgrn_norm, w_out, ffn2_norm, ffn2_w_gate, ffn2_w_up,
           ffn2_w_down):
    bsz, seq, d = x.shape
    depth = w_in.shape[0]
    n = bsz * seq
    h = x.astype(F32).reshape(n, d)
    bias = attention_bias(rel_bias)
    lb_cum = jnp.cumsum(jax.nn.softmax(hgrn_lb_logits.astype(F32), axis=1), axis=1)
    lb_all = lb_cum - lb_cum[:, :1]
    for l in range(depth):
        h = ffn(h, ffn1_norm[l], ffn1_w_gate[l].astype(BF16), ffn1_w_up[l].astype(BF16),
                ffn1_w_down[l].astype(BF16))
        qkv, rest = inproj(h, mix_norm[l], w_in[l].astype(BF16), q_norm[l], k_norm[l])
        qkv = qkv.reshape(bsz, seq, -1)
        rest = rest.reshape(bsz, seq, -1)
        y_att = attention(qkv, bias)
        y_pool = pool_mixer(rest, _pool_blockdiag(pool_w[l]).astype(BF16), pool_scale[l])
        o_fwd, o_bwd = hgrn_scan(rest, lb_all[0, l], lb_all[1, l])
        h = outproj(h, y_att.reshape(n, -1), y_pool.reshape(n, -1), o_fwd.reshape(n, -1),
                    o_bwd.reshape(n, -1), rest.reshape(n, -1), hgrn_norm[l], w_out[l].astype(BF16))
        h = ffn(h, ffn2_norm[l], ffn2_w_gate[l].astype(BF16), ffn2_w_up[l].astype(BF16),
                ffn2_w_down[l].astype(BF16))
    return h.reshape(bsz, seq, d).astype(x.dtype)
```

```python
import functools
import math

import numpy as np
import jax
import jax.numpy as jnp
from jax import lax
from jax.experimental import pallas as pl
from jax.experimental.pallas import tpu as pltpu

F32 = jnp.float32
BF16 = jnp.bfloat16

LANES = 128
SUBLANES = 8
VMEM_BYTES_V7X = 64 * 1024 * 1024
VMEM_LIMIT = VMEM_BYTES_V7X - 8 * 1024 * 1024

HEAD_DIM = 64
HEADS_PER_TILE = LANES // HEAD_DIM
ATT_WIDTH = 512
POOL_WIDTH = 256
HGRN_WIDTH = 256
POOL_WINDOWS = (2, 4, 8, 16)
DILATIONS = (1, 4, 16)
ATT_SIDE = 64
NUM_BUCKETS = 32
MAX_DISTANCE = 1024
EPS = 1e-6
NEG = -1e30

Q_BLOCK = 128
K_BLOCK = Q_BLOCK + 2 * ATT_SIDE
ATT_UNROLL = 4
HGRN_CHUNK = 64
HGRN_SUB = SUBLANES
HGRN_LEVELS = 3


def _cparams(sem):
    return pltpu.CompilerParams(dimension_semantics=sem, vmem_limit_bytes=VMEM_LIMIT)


def _resident(shape):
    nd = len(shape)
    return pl.BlockSpec(shape, lambda *_: (0,) * nd, pipeline_mode=pl.Buffered(1))


def _nt(a, b):
    return lax.dot_general(a, b, (((1,), (1,)), ((), ())), preferred_element_type=F32)


def _tn(a, b):
    return lax.dot_general(a, b, (((0,), (0,)), ((), ())), preferred_element_type=F32)


def _head_blockdiag():
    r = np.arange(LANES) // HEAD_DIM
    return jnp.asarray(r[:, None] == r[None, :], dtype=BF16)


def _head_mean_sq(v, bd):
    return jnp.dot((v * v).astype(BF16), bd, preferred_element_type=F32) * (1.0 / HEAD_DIM)


def _ffn_body(x_ref, g_ref, wg_ref, wu_ref, wd_ref, o_ref, a_ref, *, ff_chunk):
    x = x_ref[...]
    h = (x * lax.rsqrt(jnp.mean(x * x, axis=-1, keepdims=True) + EPS) * g_ref[...]).astype(BF16)
    for c in range(wg_ref.shape[1] // ff_chunk):
        sl = slice(c * ff_chunk, (c + 1) * ff_chunk)
        gate = jnp.dot(h, wg_ref[:, sl], preferred_element_type=F32)
        up = jnp.dot(h, wu_ref[:, sl], preferred_element_type=F32)
        a_ref[:, sl] = (gate * jax.nn.sigmoid(gate) * up).astype(BF16)
    o_ref[...] = x + 0.5 * jnp.dot(a_ref[...], wd_ref[...], preferred_element_type=F32)


def ffn(x2, gain, wg, wu, wd, *, tm=512, ff_chunk=256):
    n, d = x2.shape
    dff = wg.shape[1]
    return pl.pallas_call(
        functools.partial(_ffn_body, ff_chunk=ff_chunk),
        grid=(n // tm,),
        in_specs=[pl.BlockSpec((tm, d), lambda i: (i, 0)),
                  _resident((1, d)), _resident((d, dff)), _resident((d, dff)), _resident((dff, d))],
        out_specs=pl.BlockSpec((tm, d), lambda i: (i, 0)),
        out_shape=jax.ShapeDtypeStruct((n, d), F32),
        scratch_shapes=[pltpu.VMEM((tm, dff), BF16)],
        compiler_params=_cparams(("parallel",)),
    )(x2, gain.reshape(1, d), wg, wu, wd)


def _inproj_body(x_ref, g_ref, w_ref, qg_ref, kg_ref, bd_ref, qkv_ref, rest_ref):
    x = x_ref[...]
    h = (x * lax.rsqrt(jnp.mean(x * x, axis=-1, keepdims=True) + EPS) * g_ref[...]).astype(BF16)
    bd = bd_ref[...]
    for c in range(2 * ATT_WIDTH // LANES):
        sl = slice(c * LANES, (c + 1) * LANES)
        z = jnp.dot(h, w_ref[:, sl], preferred_element_type=F32)
        if c < ATT_WIDTH // LANES:
            gain = qg_ref[:, sl] * (HEAD_DIM ** -0.5)
        else:
            ck = c - ATT_WIDTH // LANES
            gain = kg_ref[:, ck * LANES:(ck + 1) * LANES]
        qkv_ref[:, sl] = z * lax.rsqrt(_head_mean_sq(z, bd) + EPS) * gain
    v0 = 2 * ATT_WIDTH
    for c in range(ATT_WIDTH // 256):
        sl = slice(v0 + c * 256, v0 + (c + 1) * 256)
        qkv_ref[:, sl] = jnp.dot(h, w_ref[:, sl], preferred_element_type=F32)
    r0 = 3 * ATT_WIDTH
    for c in range(rest_ref.shape[1] // 256):
        rest_ref[:, c * 256:(c + 1) * 256] = jnp.dot(
            h, w_ref[:, r0 + c * 256:r0 + (c + 1) * 256], preferred_element_type=F32)


def inproj(x2, gain, w_in, q_gain, k_gain, *, tm=512):
    n, d = x2.shape
    cols = w_in.shape[1]
    n_att = 3 * ATT_WIDTH
    return pl.pallas_call(
        _inproj_body,
        grid=(n // tm,),
        in_specs=[pl.BlockSpec((tm, d), lambda i: (i, 0)),
                  _resident((1, d)), _resident((d, cols)),
                  _resident((1, ATT_WIDTH)), _resident((1, ATT_WIDTH)), _resident((LANES, LANES))],
        out_specs=[pl.BlockSpec((tm, n_att), lambda i: (i, 0)),
                   pl.BlockSpec((tm, cols - n_att), lambda i: (i, 0))],
        out_shape=[jax.ShapeDtypeStruct((n, n_att), F32),
                   jax.ShapeDtypeStruct((n, cols - n_att), F32)],
        compiler_params=_cparams(("parallel",)),
    )(x2, gain.reshape(1, d), w_in, q_gain.reshape(1, ATT_WIDTH), k_gain.reshape(1, ATT_WIDTH),
      _head_blockdiag())


def _t5_bucket(rel):
    half = NUM_BUCKETS // 2
    max_exact = half // 2
    base = jnp.where(rel > 0, half, 0)
    n = jnp.abs(rel)
    nf = jnp.maximum(n, 1).astype(F32)
    large = max_exact + (jnp.log(nf / max_exact) / math.log(MAX_DISTANCE / max_exact)
                         * (half - max_exact)).astype(jnp.int32)
    large = jnp.minimum(large, half - 1)
    return base + jnp.where(n < max_exact, n, large)


def _bias_buckets():
    qi = jnp.arange(Q_BLOCK)[:, None]
    ki = jnp.arange(K_BLOCK)[None, :]
    out = []
    for dil in DILATIONS:
        per_var = []
        for var in range(3):
            rel = ki - var * ATT_SIDE - qi
            per_var.append(jnp.where(jnp.abs(rel) <= ATT_SIDE, _t5_bucket(rel * dil), -1))
        out.append(jnp.stack(per_var))
    return jnp.stack(out).astype(jnp.int32)


def _bias_body(tbl_ref, bk_ref, o_ref):
    hp = pl.program_id(0)
    for var in range(3):
        bk = bk_ref[var]
        for h in range(HEADS_PER_TILE):
            head = hp * HEADS_PER_TILE + h
            acc = jnp.full(bk.shape, NEG, F32)
            for b in range(NUM_BUCKETS):
                acc = jnp.where(bk == b, tbl_ref[b, head], acc)
            o_ref[var, h] = acc


def attention_bias(rel_bias):
    n_pairs = rel_bias.shape[1] // HEADS_PER_TILE
    n_g = len(DILATIONS)
    return pl.pallas_call(
        _bias_body,
        grid=(n_pairs, n_g),
        in_specs=[pl.BlockSpec(memory_space=pltpu.SMEM),
                  pl.BlockSpec((None, 3, Q_BLOCK, K_BLOCK), lambda p, g: (g, 0, 0, 0))],
        out_specs=pl.BlockSpec((None, None, 3, HEADS_PER_TILE, Q_BLOCK, K_BLOCK),
                               lambda p, g: (p, g, 0, 0, 0, 0)),
        out_shape=jax.ShapeDtypeStruct((n_pairs, n_g, 3, HEADS_PER_TILE, Q_BLOCK, K_BLOCK), F32),
        compiler_params=_cparams(("parallel", "parallel")),
    )(rel_bias.astype(F32), _bias_buckets())


def _attn_body(q_ref, k_ref, v_ref, bias_ref, o_ref, m_ref, l_ref):
    seq = q_ref.shape[0]
    lo = lax.broadcasted_iota(jnp.int32, (Q_BLOCK, LANES), 1) < HEAD_DIM
    klo = lax.broadcasted_iota(jnp.int32, (K_BLOCK, LANES), 1) < HEAD_DIM

    def rows(start, n, dil):
        return pl.ds(start, n) if dil == 1 else pl.ds(start, n, stride=dil)

    def scores(g, dil, sub_len, i):
        n_blocks = sub_len // Q_BLOCK
        r, n = (0, i) if dil == 1 else (i % dil, i // dil)
        qs = n * Q_BLOCK
        ks = jnp.clip(qs - ATT_SIDE, 0, sub_len - K_BLOCK)
        var = (n > 0).astype(jnp.int32) + (n == n_blocks - 1).astype(jnp.int32)
        if dil == 1:
            qs, ks = pl.multiple_of(qs, Q_BLOCK), pl.multiple_of(ks, ATT_SIDE)
        qrows = rows(r + dil * qs, Q_BLOCK, dil)
        krows = rows(r + dil * ks, K_BLOCK, dil)
        q = q_ref[qrows, :]
        k = k_ref[krows, :].astype(BF16)
        qq = jnp.concatenate([jnp.where(lo, q, 0.0), jnp.where(lo, 0.0, q)], axis=0).astype(BF16)
        return qrows, krows, _nt(qq, k) + bias_ref[g, var]

    def weighted(qrows, krows, s):
        v = v_ref[krows, :]
        m2 = jnp.max(s, axis=1, keepdims=True)
        p = jnp.exp(s - m2).astype(BF16)
        p = jnp.concatenate([p[:Q_BLOCK], p[Q_BLOCK:]], axis=1)
        vo = jnp.concatenate(
            [jnp.concatenate([jnp.where(klo, v, 0.0), jnp.where(klo, 1.0, 0.0)], axis=1),
             jnp.concatenate([jnp.where(klo, 0.0, v), jnp.where(klo, 0.0, 1.0)], axis=1)],
            axis=0).astype(BF16)
        ol = jnp.dot(p, vo, preferred_element_type=F32)
        m = jnp.where(lo, jnp.broadcast_to(m2[:Q_BLOCK], (Q_BLOCK, LANES)),
                      jnp.broadcast_to(m2[Q_BLOCK:], (Q_BLOCK, LANES)))
        return qrows, ol[:, :LANES], ol[:, LANES:], m

    def merge(g, qrows, o, l, m):
        if g > 0:
            m_old = m_ref[qrows, :]
            m_new = jnp.maximum(m_old, m)
            a_old, a_cur = jnp.exp(m_old - m_new), jnp.exp(m - m_new)
            o = o_ref[qrows, :] * a_old + o * a_cur
            l = l_ref[qrows, :] * a_old + l * a_cur
            m = m_new
        o_ref[qrows, :] = o
        l_ref[qrows, :] = l
        m_ref[qrows, :] = m

    for g, dil in enumerate(DILATIONS):
        sub_len = seq // dil
        n_total = seq // Q_BLOCK

        def step(it, carry, g=g, dil=dil, sub_len=sub_len):
            scored = [scores(g, dil, sub_len, it * ATT_UNROLL + u) for u in range(ATT_UNROLL)]
            done = [weighted(*sc) for sc in scored]
            for res in done:
                merge(g, *res)
            return carry

        lax.fori_loop(0, n_total // ATT_UNROLL, step, 0)

    def normalise(i, c):
        sl = pl.ds(pl.multiple_of(i * Q_BLOCK, Q_BLOCK), Q_BLOCK)
        o_ref[sl, :] = o_ref[sl, :] / l_ref[sl, :]
        return c

    lax.fori_loop(0, seq // Q_BLOCK, normalise, 0)


def attention(qkv, bias):
    bsz, seq, _ = qkv.shape
    n_pairs = ATT_WIDTH // LANES
    assert seq % (max(DILATIONS) * Q_BLOCK) == 0 and seq // max(DILATIONS) >= K_BLOCK
    assert (seq // Q_BLOCK) % ATT_UNROLL == 0 and all(d == 1 or d % ATT_UNROLL == 0 for d in DILATIONS)
    bias = bias.reshape(bias.shape[:3] + (HEADS_PER_TILE * Q_BLOCK, K_BLOCK))

    def col(c0):
        return pl.BlockSpec((None, seq, LANES), lambda b, p: (b, 0, c0 + p))

    return pl.pallas_call(
        _attn_body,
        grid=(bsz, n_pairs),
        in_specs=[col(0), col(n_pairs), col(2 * n_pairs),
                  pl.BlockSpec((None,) + bias.shape[1:], lambda b, p: (p, 0, 0, 0, 0))],
        out_specs=pl.BlockSpec((None, seq, LANES), lambda b, p: (b, 0, p)),
        out_shape=jax.ShapeDtypeStruct((bsz, seq, ATT_WIDTH), F32),
        scratch_shapes=[pltpu.VMEM((seq, LANES), F32), pltpu.VMEM((seq, LANES), F32)],
        compiler_params=_cparams(("parallel", "parallel")),
    )(qkv, qkv, qkv, bias)


POOL_HALO = max(POOL_WINDOWS) // 2


def _pool_body(u_ref, prev_ref, next_ref, w_ref, scale_ref, o_ref, pad_ref, *, seq):
    t = pl.program_id(1)
    ts = u_ref.shape[0]
    u = u_ref[...]
    pad_ref[pl.ds(0, POOL_HALO), :] = jnp.where(t > 0, prev_ref[...], 0.0)
    pad_ref[pl.ds(POOL_HALO, ts), :] = u
    pad_ref[pl.ds(POOL_HALO + ts, POOL_HALO), :] = jnp.where(t < pl.num_programs(1) - 1, next_ref[...], 0.0)

    def shifted(off):
        return pad_ref[pl.ds(POOL_HALO + off, ts), :]

    lane_win = lax.broadcasted_iota(jnp.int32, (1, POOL_WIDTH), 1) // HEAD_DIM
    pos = t * ts + lax.broadcasted_iota(jnp.int32, (ts, 1), 0)
    total = jnp.zeros_like(u)
    cnt = jnp.zeros_like(u)
    acc = None
    lo_off, hi_off = 0, 0
    for gi, win in enumerate(POOL_WINDOWS):
        for off in list(range(-(win // 2), lo_off)) + list(range(hi_off, win - win // 2)):
            acc = shifted(off) if acc is None else acc + shifted(off)
        lo_off, hi_off = -(win // 2), win - win // 2
        in_group = lane_win == gi
        n = (jnp.minimum(pos + hi_off, seq) - jnp.maximum(pos + lo_off, 0)).astype(F32)
        total = jnp.where(in_group, acc, total)
        cnt = jnp.where(in_group, n, cnt)
    mixed = jnp.dot((total / cnt - u).astype(BF16), w_ref[...], preferred_element_type=F32)
    o_ref[...] = mixed * scale_ref[...]


def pool_mixer(rest, w_blockdiag, scale, *, ts=1024):
    bsz, seq, _ = rest.shape
    ts = min(ts, seq)
    per_tile = ts // POOL_HALO
    last_halo = seq // POOL_HALO - 1
    return pl.pallas_call(
        functools.partial(_pool_body, seq=seq),
        grid=(bsz, seq // ts),
        in_specs=[pl.BlockSpec((None, ts, POOL_WIDTH), lambda b, t: (b, t, 0)),
                  pl.BlockSpec((None, POOL_HALO, POOL_WIDTH),
                               lambda b, t: (b, jnp.maximum(t * per_tile - 1, 0), 0)),
                  pl.BlockSpec((None, POOL_HALO, POOL_WIDTH),
                               lambda b, t: (b, jnp.minimum((t + 1) * per_tile, last_halo), 0)),
                  _resident((POOL_WIDTH, POOL_WIDTH)), _resident((1, POOL_WIDTH))],
        out_specs=pl.BlockSpec((None, ts, POOL_WIDTH), lambda b, t: (b, t, 0)),
        out_shape=jax.ShapeDtypeStruct((bsz, seq, POOL_WIDTH), F32),
        scratch_shapes=[pltpu.VMEM((ts + 2 * POOL_HALO, POOL_WIDTH), F32)],
        compiler_params=_cparams(("parallel", "parallel")),
    )(rest, rest, rest, w_blockdiag, scale.reshape(1, POOL_WIDTH))


def _pool_blockdiag(w_pool):
    n_g = w_pool.shape[0]
    eye = jnp.eye(n_g, dtype=w_pool.dtype)
    return jnp.einsum('gcd,gh->gchd', w_pool, eye).reshape(n_g * HEAD_DIM, n_g * HEAD_DIM)


def _hgrn_constants():
    c = HGRN_CHUNK
    t = np.arange(c)
    s_col = np.tile(t, HEADS_PER_TILE)
    cum, masks = [], []
    for direction in range(2):
        tri = (t[None, :] <= t[:, None]) if direction == 0 else (t[None, :] >= t[:, None])
        tri = tri.astype(np.float32)
        mats, qms, bms = [tri], [], []
        for level in range(HGRN_LEVELS):
            half = HGRN_SUB << level
            start = (t // (2 * half)) * (2 * half)
            boundary = start + (half - 1 if direction == 0 else half)
            mats.append(tri[boundary])
            q_right = (t // half) % 2 == 1
            qm = q_right if direction == 0 else ~q_right
            qms.append(np.broadcast_to(qm[:, None], (c, LANES)))
            bms.append(t[:, None] // (2 * half) == s_col[None, :] // (2 * half))
        same_sub = t[:, None] // HGRN_SUB == s_col[None, :] // HGRN_SUB
        causal = (s_col[None, :] <= t[:, None]) if direction == 0 else (s_col[None, :] >= t[:, None])
        cum.append(np.concatenate(mats, axis=0))
        masks.append(np.stack(qms + bms[:HGRN_LEVELS - 1] + [same_sub & causal]).astype(np.float32))
    lane_head = np.arange(LANES) // HEAD_DIM
    sel = np.stack([(lane_head[:, None] == lane_head[None, :]) & (s_col[None, :] % HGRN_SUB == j)
                    for j in range(HGRN_SUB)]).astype(np.float32)
    return (jnp.asarray(np.stack(cum), BF16), jnp.asarray(np.stack(masks), F32), jnp.asarray(sel, BF16))


def _split3(x):
    x1 = x.astype(BF16)
    r1 = x - x1.astype(F32)
    x2 = r1.astype(BF16)
    x3 = (r1 - x2.astype(F32)).astype(BF16)
    return x1, x2, x3


def _by_head(x):
    lo = lax.broadcasted_iota(jnp.int32, x.shape, 1) < HEAD_DIM
    return jnp.concatenate([jnp.where(lo, x, 0.0), jnp.where(lo, 0.0, x)], axis=0).astype(BF16)


def _hgrn_gates(st, cum_ref):
    c = HGRN_CHUNK
    st["q"], st["v"] = st["q_ref"][st["rows"], st["cols"]], st["v_ref"][st["rows"], st["cols"]]
    fl = st["f_ref"][st["rows"], st["cols"]]
    lb = st["lb_ref"][:, st["cols"]]
    e = jnp.exp(-jnp.abs(fl))
    log_sig = jnp.minimum(fl, 0.0) - jnp.log1p(e)
    rcp = 1.0 / (1.0 + e)
    st["kk"] = (1.0 - lb) * jnp.where(fl >= 0, e * rcp, rcp)
    log_lb = jnp.log(lb)
    cc = jnp.log1p(-lb) + log_sig
    g = jnp.maximum(log_lb, cc) + jnp.log1p(jnp.exp(-jnp.abs(log_lb - cc)))
    cum = cum_ref[st["direction"]]
    st["ball"] = sum(jnp.dot(cum, part, preferred_element_type=F32) for part in _split3(g))


def _hgrn_products(st, mask_ref, sel_ref, state_ref, b_scr, k_scr):
    c = HGRN_CHUNK
    direction, stream = st["direction"], st["stream"]
    q, v, kk, ball = st["q"], st["v"], st["kk"], st["ball"]
    b = ball[0:c]
    b_edge = b[c - 1:c] if direction == 0 else b[0:1]

    state = state_ref[stream]
    st["o_state"] = _nt((q * jnp.exp(b)).astype(BF16), state.astype(BF16))
    k_out = kk * jnp.exp(b_edge - b)
    st["upd"] = _tn(v.astype(BF16), k_out.astype(BF16))
    st["decayed"] = state * jnp.exp(b_edge)

    att = None
    for level in range(HGRN_LEVELS):
        beta = ball[(level + 1) * c:(level + 2) * c]
        qm = mask_ref[direction, level]
        ql = q * jnp.exp(jnp.minimum(b - beta, 0.0)) * qm
        kl = kk * jnp.exp(jnp.minimum(beta - b, 0.0)) * (1.0 - qm)
        a = _nt(ql.astype(BF16), _by_head(kl))
        if level < HGRN_LEVELS - 1:
            a = a * mask_ref[direction, HGRN_LEVELS + level]
        att = a if att is None else att + a

    b_scr[stream] = b
    k_scr[stream] = kk
    diag = None
    for j in range(HGRN_SUB):
        pieces = []
        for u in range(c // HGRN_SUB):
            r = u * HGRN_SUB
            b_row = b_scr[stream, pl.ds(r + j, 1), :]
            k_row = k_scr[stream, pl.ds(r + j, 1), :]
            pieces.append((q[r:r + HGRN_SUB] * k_row)
                          * jnp.exp(jnp.minimum(b[r:r + HGRN_SUB] - b_row, 0.0)))
        term = jnp.dot(jnp.concatenate(pieces, axis=0).astype(BF16), sel_ref[j],
                       preferred_element_type=F32)
        diag = term if diag is None else diag + term
    st["att"] = att + diag * mask_ref[direction, 2 * HGRN_LEVELS - 1]


def _hgrn_finish(st, state_ref):
    r128 = lax.broadcasted_iota(jnp.int32, (LANES, LANES), 0) // HEAD_DIM
    c128 = lax.broadcasted_iota(jnp.int32, (LANES, LANES), 1) // HEAD_DIM
    state_ref[st["stream"]] = jnp.where(r128 == c128, st["decayed"] + st["upd"], 0.0)
    st["o_ref"][st["rows"], st["cols"]] = st["o_state"] + jnp.dot(
        st["att"].astype(BF16), _by_head(st["v"]), preferred_element_type=F32)


def _hgrn_body(qf_ref, vf_ref, ff_ref, qb_ref, vb_ref, fb_ref, lbf_ref, lbb_ref,
               cum_ref, mask_ref, sel_ref, of_ref, ob_ref, state_ref, b_scr, k_scr):
    @pl.when(pl.program_id(1) == 0)
    def _():
        state_ref[...] = jnp.zeros_like(state_ref)

    n_chunks = qf_ref.shape[0] // HGRN_CHUNK
    n_tiles = HGRN_WIDTH // LANES
    per_direction = ((qf_ref, vf_ref, ff_ref, lbf_ref, of_ref), (qb_ref, vb_ref, fb_ref, lbb_ref, ob_ref))

    def step(i, carry):
        streams = []
        for direction, (q_ref, v_ref, f_ref, lb_ref, o_ref) in enumerate(per_direction):
            chunk = i if direction == 0 else n_chunks - 1 - i
            rows = pl.ds(pl.multiple_of(chunk * HGRN_CHUNK, HGRN_CHUNK), HGRN_CHUNK)
            for tile in range(n_tiles):
                streams.append(dict(direction=direction, stream=direction * n_tiles + tile, rows=rows,
                                    cols=slice(tile * LANES, (tile + 1) * LANES), q_ref=q_ref,
                                    v_ref=v_ref, f_ref=f_ref, lb_ref=lb_ref, o_ref=o_ref))
        for st in streams:
            _hgrn_gates(st, cum_ref)
        for st in streams:
            _hgrn_products(st, mask_ref, sel_ref, state_ref, b_scr, k_scr)
        for st in streams:
            _hgrn_finish(st, state_ref)
        return carry

    lax.fori_loop(0, n_chunks, step, 0)


def hgrn_scan(rest, lb_fwd, lb_bwd, *, ts=1024):
    bsz, seq, _ = rest.shape
    ts = min(ts, seq)
    nt = seq // ts
    n_streams = 2 * HGRN_WIDTH // LANES
    cum, masks, sel = _hgrn_constants()

    def fwd(c0):
        return pl.BlockSpec((None, ts, HGRN_WIDTH), lambda b, j: (b, j, c0))

    def bwd(c0):
        return pl.BlockSpec((None, ts, HGRN_WIDTH), lambda b, j: (b, nt - 1 - j, c0))

    return pl.pallas_call(
        _hgrn_body,
        grid=(bsz, nt),
        in_specs=[fwd(1), fwd(2), fwd(3), bwd(1), bwd(2), bwd(4),
                  _resident((1, HGRN_WIDTH)), _resident((1, HGRN_WIDTH)),
                  _resident(cum.shape), _resident(masks.shape), _resident(sel.shape)],
        out_specs=[fwd(0), bwd(0)],
        out_shape=[jax.ShapeDtypeStruct((bsz, seq, HGRN_WIDTH), F32)] * 2,
        scratch_shapes=[pltpu.VMEM((n_streams, LANES, LANES), F32),
                        pltpu.VMEM((n_streams, HGRN_CHUNK, LANES), F32),
                        pltpu.VMEM((n_streams, HGRN_CHUNK, LANES), F32)],
        compiler_params=_cparams(("parallel", "arbitrary")),
    )(rest, rest, rest, rest, rest, rest, lb_fwd.reshape(1, HGRN_WIDTH), lb_bwd.reshape(1, HGRN_WIDTH),
      cum, masks, sel)


def _outproj_body(x_ref, att_ref, pool_ref, of_ref, ob_ref, gate_ref, gain_ref, bd_ref, w_ref, o_ref):
    bd = bd_ref[...]
    acc = x_ref[...]
    acc = acc + jnp.dot(att_ref[...].astype(BF16), w_ref[0:ATT_WIDTH, :], preferred_element_type=F32)
    acc = acc + jnp.dot(pool_ref[...].astype(BF16), w_ref[ATT_WIDTH:ATT_WIDTH + POOL_WIDTH, :],
                        preferred_element_type=F32)
    gate = gate_ref[...]
    rec = []
    for c in range(HGRN_WIDTH // LANES):
        sl = slice(c * LANES, (c + 1) * LANES)
        o = of_ref[:, sl] + ob_ref[:, sl]
        y = o * lax.rsqrt(_head_mean_sq(o, bd) + EPS) * gain_ref[:, sl]
        gt = gate[:, sl]
        rec.append((y * (gt * jax.nn.sigmoid(gt))).astype(BF16))
    acc = acc + jnp.dot(jnp.concatenate(rec, axis=1), w_ref[ATT_WIDTH + POOL_WIDTH:, :],
                        preferred_element_type=F32)
    o_ref[...] = acc


def outproj(x2, y_att, y_pool, o_fwd, o_bwd, rest, out_gain, w_out, *, tm=512):
    n, d = x2.shape
    gate_tile = (rest.shape[1] - HGRN_WIDTH) // HGRN_WIDTH

    def rowblock(width, col=0):
        return pl.BlockSpec((tm, width), lambda i: (i, col))

    return pl.pallas_call(
        _outproj_body,
        grid=(n // tm,),
        in_specs=[rowblock(d), rowblock(ATT_WIDTH), rowblock(POOL_WIDTH), rowblock(HGRN_WIDTH),
                  rowblock(HGRN_WIDTH), rowblock(HGRN_WIDTH, gate_tile),
                  _resident((1, HGRN_WIDTH)), _resident((LANES, LANES)), _resident(w_out.shape)],
        out_specs=rowblock(d),
        out_shape=jax.ShapeDtypeStruct((n, d), F32),
        compiler_params=_cparams(("parallel",)),
    )(x2, y_att, y_pool, o_fwd, o_bwd, rest,
      jnp.tile(out_gain, HGRN_WIDTH // HEAD_DIM).reshape(1, HGRN_WIDTH), _head_blockdiag(), w_out)


def kernel(x, ffn1_norm, ffn1_w_gate, ffn1_w_up, ffn1_w_down, mix_norm, w_in, q_norm, k_norm, rel_bias,
           pool_w, pool_scale, hgrn_lb_logits, hgrn_norm, w_out, ffn2_norm, ffn2_w_gate, ffn2_w_up,
           ffn2_w_down):
    bsz, seq, d = x.shape
    depth = w_in.shape[0]
    n = bsz * seq
    h = x.astype(F32).reshape(n, d)
    bias = attention_bias(rel_bias)
    lb_cum = jnp.cumsum(jax.nn.softmax(hgrn_lb_logits.astype(F32), axis=1), axis=1)
    lb_all = lb_cum - lb_cum[:, :1]
    for l in range(depth):
        h = ffn(h, ffn1_norm[l], ffn1_w_gate[l].astype(BF16), ffn1_w_up[l].astype(BF16),
                ffn1_w_down[l].astype(BF16))
        qkv, rest = inproj(h, mix_norm[l], w_in[l].astype(BF16), q_norm[l], k_norm[l])
        qkv = qkv.reshape(bsz, seq, -1)
        rest = rest.reshape(bsz, seq, -1)
        y_att = attention(qkv, bias)
        y_pool = pool_mixer(rest, _pool_blockdiag(pool_w[l]).astype(BF16), pool_scale[l])
        o_fwd, o_bwd = hgrn_scan(rest, lb_all[0, l], lb_all[1, l])
        h = outproj(h, y_att.reshape(n, -1), y_pool.reshape(n, -1), o_fwd.reshape(n, -1),
                    o_bwd.reshape(n, -1), rest.reshape(n, -1), hgrn_norm[l], w_out[l].astype(BF16))
        h = ffn(h, ffn2_norm[l], ffn2_w_gate[l].astype(BF16), ffn2_w_up[l].astype(BF16),
                ffn2_w_down[l].astype(BF16))
    return h.reshape(bsz, seq, d).astype(x.dtype)
```

```python
import functools
import math

import numpy as np
import jax
import jax.numpy as jnp
from jax import lax
from jax.experimental import pallas as pl
from jax.experimental.pallas import tpu as pltpu

F32 = jnp.float32
BF16 = jnp.bfloat16

LANES = 128
SUBLANES = 8
MXU_COLS = 256
VMEM_BYTES_V7X = 64 * 1024 * 1024
VMEM_LIMIT = VMEM_BYTES_V7X - 8 * 1024 * 1024

HEAD_DIM = 64
HEADS_PER_TILE = LANES // HEAD_DIM
ATT_WIDTH = 512
POOL_WIDTH = 256
HGRN_WIDTH = 256
POOL_WINDOWS = (2, 4, 8, 16)
DILATIONS = (1, 4, 16)
ATT_SIDE = 64
NUM_BUCKETS = 32
MAX_DISTANCE = 1024
EPS = 1e-6
NEG = -1e30
LOG2_E = math.log2(math.e)

Q_BLOCK = 128
K_BLOCK = Q_BLOCK + 2 * ATT_SIDE
ATT_UNROLL = 4
HGRN_CHUNK = 64
HGRN_SUB = SUBLANES
HGRN_LEVELS = 3


def _cparams(sem):
    return pltpu.CompilerParams(dimension_semantics=sem, vmem_limit_bytes=VMEM_LIMIT)


def _resident(shape):
    nd = len(shape)
    return pl.BlockSpec(shape, lambda *_: (0,) * nd, pipeline_mode=pl.Buffered(1))


def _nt(a, b):
    return lax.dot_general(a, b, (((1,), (1,)), ((), ())), preferred_element_type=F32)


def _tn(a, b):
    return lax.dot_general(a, b, (((0,), (0,)), ((), ())), preferred_element_type=F32)


def _head_blockdiag(width=LANES):
    r = np.arange(width) // HEAD_DIM
    return jnp.asarray(r[:, None] == r[None, :], dtype=BF16)


def _head_mean_sq(v, bd):
    return jnp.dot((v * v).astype(BF16), bd, preferred_element_type=F32) * (1.0 / HEAD_DIM)


def _ffn_body(x_ref, g_ref, wg_ref, wu_ref, wd_ref, o_ref, a_ref, *, ff_chunk):
    x = x_ref[...]
    h = (x * lax.rsqrt(jnp.mean(x * x, axis=-1, keepdims=True) + EPS) * g_ref[...]).astype(BF16)
    for c in range(wg_ref.shape[1] // ff_chunk):
        sl = slice(c * ff_chunk, (c + 1) * ff_chunk)
        gate = jnp.dot(h, wg_ref[:, sl], preferred_element_type=F32)
        up = jnp.dot(h, wu_ref[:, sl], preferred_element_type=F32)
        a_ref[:, sl] = (gate * jax.nn.sigmoid(gate) * up).astype(BF16)
    o_ref[...] = x + 0.5 * jnp.dot(a_ref[...], wd_ref[...], preferred_element_type=F32)


def ffn(x2, gain, wg, wu, wd, *, tm=512, ff_chunk=256):
    n, d = x2.shape
    dff = wg.shape[1]
    return pl.pallas_call(
        functools.partial(_ffn_body, ff_chunk=ff_chunk),
        grid=(n // tm,),
        in_specs=[pl.BlockSpec((tm, d), lambda i: (i, 0)),
                  _resident((1, d)), _resident((d, dff)), _resident((d, dff)), _resident((dff, d))],
        out_specs=pl.BlockSpec((tm, d), lambda i: (i, 0)),
        out_shape=jax.ShapeDtypeStruct((n, d), F32),
        scratch_shapes=[pltpu.VMEM((tm, dff), BF16)],
        compiler_params=_cparams(("parallel",)),
    )(x2, gain.reshape(1, d), wg, wu, wd)


def _inproj_body(x_ref, g_ref, w_ref, qg_ref, kg_ref, bd_ref, qkv_ref, rest_ref):
    x = x_ref[...]
    h = (x * lax.rsqrt(jnp.mean(x * x, axis=-1, keepdims=True) + EPS) * g_ref[...]).astype(BF16)
    bd = bd_ref[...]
    wc = MXU_COLS
    for c in range(2 * ATT_WIDTH // wc):
        sl = slice(c * wc, (c + 1) * wc)
        z = jnp.dot(h, w_ref[:, sl], preferred_element_type=F32)
        if c < ATT_WIDTH // wc:
            gain = qg_ref[:, sl] * (HEAD_DIM ** -0.5)
        else:
            gain = kg_ref[:, c * wc - ATT_WIDTH:(c + 1) * wc - ATT_WIDTH]
        qkv_ref[:, sl] = z * lax.rsqrt(_head_mean_sq(z, bd) + EPS) * gain
    v0 = 2 * ATT_WIDTH
    for c in range(ATT_WIDTH // wc):
        sl = slice(v0 + c * wc, v0 + (c + 1) * wc)
        qkv_ref[:, sl] = jnp.dot(h, w_ref[:, sl], preferred_element_type=F32)
    r0 = 3 * ATT_WIDTH
    for c in range(rest_ref.shape[1] // wc):
        rest_ref[:, c * wc:(c + 1) * wc] = jnp.dot(
            h, w_ref[:, r0 + c * wc:r0 + (c + 1) * wc], preferred_element_type=F32)


def inproj(x2, gain, w_in, q_gain, k_gain, *, tm=512):
    n, d = x2.shape
    cols = w_in.shape[1]
    n_att = 3 * ATT_WIDTH
    return pl.pallas_call(
        _inproj_body,
        grid=(n // tm,),
        in_specs=[pl.BlockSpec((tm, d), lambda i: (i, 0)),
                  _resident((1, d)), _resident((d, cols)),
                  _resident((1, ATT_WIDTH)), _resident((1, ATT_WIDTH)), _resident((MXU_COLS, MXU_COLS))],
        out_specs=[pl.BlockSpec((tm, n_att), lambda i: (i, 0)),
                   pl.BlockSpec((tm, cols - n_att), lambda i: (i, 0))],
        out_shape=[jax.ShapeDtypeStruct((n, n_att), F32),
                   jax.ShapeDtypeStruct((n, cols - n_att), F32)],
        compiler_params=_cparams(("parallel",)),
    )(x2, gain.reshape(1, d), w_in, q_gain.reshape(1, ATT_WIDTH), k_gain.reshape(1, ATT_WIDTH),
      _head_blockdiag(MXU_COLS))


def _t5_bucket(rel):
    half = NUM_BUCKETS // 2
    max_exact = half // 2
    base = jnp.where(rel > 0, half, 0)
    n = jnp.abs(rel)
    nf = jnp.maximum(n, 1).astype(F32)
    large = max_exact + (jnp.log(nf / max_exact) / math.log(MAX_DISTANCE / max_exact)
                         * (half - max_exact)).astype(jnp.int32)
    large = jnp.minimum(large, half - 1)
    return base + jnp.where(n < max_exact, n, large)


def _bias_buckets():
    qi = jnp.arange(Q_BLOCK)[:, None]
    ki = jnp.arange(K_BLOCK)[None, :]
    out = []
    for dil in DILATIONS:
        per_var = []
        for var in range(3):
            rel = ki - var * ATT_SIDE - qi
            per_var.append(jnp.where(jnp.abs(rel) <= ATT_SIDE, _t5_bucket(rel * dil), -1))
        out.append(jnp.stack(per_var))
    return jnp.stack(out).astype(jnp.int32)


def _bias_body(tbl_ref, bk_ref, o_ref):
    hp = pl.program_id(0)
    for var in range(3):
        bk = bk_ref[var]
        for h in range(HEADS_PER_TILE):
            head = hp * HEADS_PER_TILE + h
            acc = jnp.full(bk.shape, NEG, F32)
            for b in range(NUM_BUCKETS):
                acc = jnp.where(bk == b, tbl_ref[b, head], acc)
            o_ref[var, h] = acc


def attention_bias(rel_bias):
    n_pairs = rel_bias.shape[1] // HEADS_PER_TILE
    n_g = len(DILATIONS)
    return pl.pallas_call(
        _bias_body,
        grid=(n_pairs, n_g),
        in_specs=[pl.BlockSpec(memory_space=pltpu.SMEM),
                  pl.BlockSpec((None, 3, Q_BLOCK, K_BLOCK), lambda p, g: (g, 0, 0, 0))],
        out_specs=pl.BlockSpec((None, None, 3, HEADS_PER_TILE, Q_BLOCK, K_BLOCK),
                               lambda p, g: (p, g, 0, 0, 0, 0)),
        out_shape=jax.ShapeDtypeStruct((n_pairs, n_g, 3, HEADS_PER_TILE, Q_BLOCK, K_BLOCK), F32),
        compiler_params=_cparams(("parallel", "parallel")),
    )(rel_bias.astype(F32), _bias_buckets())


def _attn_body(q_ref, k_ref, v_ref, bias_ref, o_ref, m_ref, l_ref):
    seq = q_ref.shape[0]
    lo = lax.broadcasted_iota(jnp.int32, (Q_BLOCK, LANES), 1) < HEAD_DIM
    klo = lax.broadcasted_iota(jnp.int32, (K_BLOCK, LANES), 1) < HEAD_DIM

    def rows(start, n, dil):
        return pl.ds(start, n) if dil == 1 else pl.ds(start, n, stride=dil)

    def scores(g, dil, sub_len, i):
        n_blocks = sub_len // Q_BLOCK
        r, n = (0, i) if dil == 1 else (i % dil, i // dil)
        qs = n * Q_BLOCK
        ks = jnp.clip(qs - ATT_SIDE, 0, sub_len - K_BLOCK)
        var = jnp.where(n > 0, 1, 0) + jnp.where(n == n_blocks - 1, 1, 0)
        if dil == 1:
            qs, ks = pl.multiple_of(qs, Q_BLOCK), pl.multiple_of(ks, ATT_SIDE)
        qrows = rows(r + dil * qs, Q_BLOCK, dil)
        krows = rows(r + dil * ks, K_BLOCK, dil)
        q = q_ref[qrows, :]
        k = k_ref[krows, :].astype(BF16)
        qq = jnp.concatenate([jnp.where(lo, q, 0.0), jnp.where(lo, 0.0, q)], axis=0).astype(BF16)
        return qrows, krows, _nt(qq, k) + bias_ref[g, var]

    def weighted(qrows, krows, s):
        v = v_ref[krows, :]
        m2 = jnp.max(s, axis=1, keepdims=True)
        p = jnp.exp(s - m2).astype(BF16)
        p = jnp.concatenate([p[:Q_BLOCK], p[Q_BLOCK:]], axis=1)
        vo = jnp.concatenate(
            [jnp.concatenate([jnp.where(klo, v, 0.0), jnp.where(klo, 1.0, 0.0)], axis=1),
             jnp.concatenate([jnp.where(klo, 0.0, v), jnp.where(klo, 0.0, 1.0)], axis=1)],
            axis=0).astype(BF16)
        ol = jnp.dot(p, vo, preferred_element_type=F32)
        m = jnp.where(lo, jnp.broadcast_to(m2[:Q_BLOCK], (Q_BLOCK, LANES)),
                      jnp.broadcast_to(m2[Q_BLOCK:], (Q_BLOCK, LANES)))
        return qrows, ol[:, :LANES], ol[:, LANES:], m

    def merge(g, qrows, o, l, m):
        if g > 0:
            m_old = m_ref[qrows, :]
            m_new = jnp.maximum(m_old, m)
            a_old, a_cur = jnp.exp(m_old - m_new), jnp.exp(m - m_new)
            o = o_ref[qrows, :] * a_old + o * a_cur
            l = l_ref[qrows, :] * a_old + l * a_cur
            m = m_new
        o_ref[qrows, :] = o
        l_ref[qrows, :] = l
        m_ref[qrows, :] = m

    for g, dil in enumerate(DILATIONS):
        sub_len = seq // dil
        n_total = seq // Q_BLOCK

        def step(it, carry, g=g, dil=dil, sub_len=sub_len):
            scored = [scores(g, dil, sub_len, it * ATT_UNROLL + u) for u in range(ATT_UNROLL)]
            done = [weighted(*sc) for sc in scored]
            for res in done:
                merge(g, *res)
            return carry

        lax.fori_loop(0, n_total // ATT_UNROLL, step, 0)

    def normalise(i, c):
        sl = pl.ds(pl.multiple_of(i * Q_BLOCK, Q_BLOCK), Q_BLOCK)
        o_ref[sl, :] = o_ref[sl, :] / l_ref[sl, :]
        return c

    lax.fori_loop(0, seq // Q_BLOCK, normalise, 0)


def attention(qkv, bias):
    bsz, seq, _ = qkv.shape
    n_pairs = ATT_WIDTH // LANES
    assert seq % (max(DILATIONS) * Q_BLOCK) == 0 and seq // max(DILATIONS) >= K_BLOCK
    assert (seq // Q_BLOCK) % ATT_UNROLL == 0 and all(d == 1 or d % ATT_UNROLL == 0 for d in DILATIONS)
    bias = bias.reshape(bias.shape[:3] + (HEADS_PER_TILE * Q_BLOCK, K_BLOCK))

    def col(c0):
        return pl.BlockSpec((None, seq, LANES), lambda b, p: (b, 0, c0 + p))

    return pl.pallas_call(
        _attn_body,
        grid=(bsz, n_pairs),
        in_specs=[col(0), col(n_pairs), col(2 * n_pairs),
                  pl.BlockSpec((None,) + bias.shape[1:], lambda b, p: (p, 0, 0, 0, 0))],
        out_specs=pl.BlockSpec((None, seq, LANES), lambda b, p: (b, 0, p)),
        out_shape=jax.ShapeDtypeStruct((bsz, seq, ATT_WIDTH), F32),
        scratch_shapes=[pltpu.VMEM((seq, LANES), F32), pltpu.VMEM((seq, LANES), F32)],
        compiler_params=_cparams(("parallel", "parallel")),
    )(qkv, qkv, qkv, bias)


POOL_HALO = max(POOL_WINDOWS) // 2


def _pool_body(u_ref, prev_ref, next_ref, w_ref, scale_ref, o_ref, pad_ref, *, seq):
    t = pl.program_id(1)
    ts = u_ref.shape[0]
    u = u_ref[...]
    pad_ref[pl.ds(0, POOL_HALO), :] = jnp.where(t > 0, prev_ref[...], 0.0)
    pad_ref[pl.ds(POOL_HALO, ts), :] = u
    pad_ref[pl.ds(POOL_HALO + ts, POOL_HALO), :] = jnp.where(t < pl.num_programs(1) - 1, next_ref[...], 0.0)

    def shifted(off):
        return pad_ref[pl.ds(POOL_HALO + off, ts), :]

    lane_win = lax.broadcasted_iota(jnp.int32, (1, POOL_WIDTH), 1) // HEAD_DIM
    pos = t * ts + lax.broadcasted_iota(jnp.int32, (ts, 1), 0)
    total = jnp.zeros_like(u)
    cnt = jnp.zeros_like(u)
    acc = None
    lo_off, hi_off = 0, 0
    for gi, win in enumerate(POOL_WINDOWS):
        for off in list(range(-(win // 2), lo_off)) + list(range(hi_off, win - win // 2)):
            acc = shifted(off) if acc is None else acc + shifted(off)
        lo_off, hi_off = -(win // 2), win - win // 2
        in_group = lane_win == gi
        n = (jnp.minimum(pos + hi_off, seq) - jnp.maximum(pos + lo_off, 0)).astype(F32)
        total = jnp.where(in_group, acc, total)
        cnt = jnp.where(in_group, n, cnt)
    mixed = jnp.dot((total / cnt - u).astype(BF16), w_ref[...], preferred_element_type=F32)
    o_ref[...] = mixed * scale_ref[...]


def pool_mixer(rest, w_blockdiag, scale, *, ts=1024):
    bsz, seq, _ = rest.shape
    ts = min(ts, seq)
    per_tile = ts // POOL_HALO
    last_halo = seq // POOL_HALO - 1
    return pl.pallas_call(
        functools.partial(_pool_body, seq=seq),
        grid=(bsz, seq // ts),
        in_specs=[pl.BlockSpec((None, ts, POOL_WIDTH), lambda b, t: (b, t, 0)),
                  pl.BlockSpec((None, POOL_HALO, POOL_WIDTH),
                               lambda b, t: (b, jnp.maximum(t * per_tile - 1, 0), 0)),
                  pl.BlockSpec((None, POOL_HALO, POOL_WIDTH),
                               lambda b, t: (b, jnp.minimum((t + 1) * per_tile, last_halo), 0)),
                  _resident((POOL_WIDTH, POOL_WIDTH)), _resident((1, POOL_WIDTH))],
        out_specs=pl.BlockSpec((None, ts, POOL_WIDTH), lambda b, t: (b, t, 0)),
        out_shape=jax.ShapeDtypeStruct((bsz, seq, POOL_WIDTH), F32),
        scratch_shapes=[pltpu.VMEM((ts + 2 * POOL_HALO, POOL_WIDTH), F32)],
        compiler_params=_cparams(("parallel", "parallel")),
    )(rest, rest, rest, w_blockdiag, scale.reshape(1, POOL_WIDTH))


def _pool_blockdiag(w_pool):
    n_g = w_pool.shape[0]
    eye = jnp.eye(n_g, dtype=w_pool.dtype)
    return jnp.einsum('gcd,gh->gchd', w_pool, eye).reshape(n_g * HEAD_DIM, n_g * HEAD_DIM)


def _hgrn_constants():
    c = HGRN_CHUNK
    t = np.arange(c)
    s_col = np.tile(t, HEADS_PER_TILE)
    cum, masks = [], []
    for direction in range(2):
        tri = (t[None, :] <= t[:, None]) if direction == 0 else (t[None, :] >= t[:, None])
        tri = tri.astype(np.float32)
        mats, qms, bms = [tri], [], []
        for level in range(HGRN_LEVELS):
            half = HGRN_SUB << level
            start = (t // (2 * half)) * (2 * half)
            boundary = start + (half - 1 if direction == 0 else half)
            mats.append(tri[boundary])
            q_right = (t // half) % 2 == 1
            qm = q_right if direction == 0 else ~q_right
            qms.append(np.broadcast_to(qm[:, None], (c, LANES)))
            bms.append(t[:, None] // (2 * half) == s_col[None, :] // (2 * half))
        same_sub = t[:, None] // HGRN_SUB == s_col[None, :] // HGRN_SUB
        causal = (s_col[None, :] <= t[:, None]) if direction == 0 else (s_col[None, :] >= t[:, None])
        cum.append(np.tile(np.concatenate(mats, axis=0), (1, 3)))
        masks.append(np.stack(qms + bms[:HGRN_LEVELS - 1] + [same_sub & causal]).astype(np.float32))
    lane_head = np.arange(LANES) // HEAD_DIM
    sel = np.concatenate([(lane_head[:, None] == lane_head[None, :]) & (s_col[None, :] % HGRN_SUB == j)
                          for j in range(HGRN_SUB)], axis=0).astype(np.float32)
    return (jnp.asarray(np.stack(cum), BF16), jnp.asarray(np.stack(masks), F32), jnp.asarray(sel, BF16))


def _split3(x):
    x1 = x.astype(BF16)
    r1 = x - x1.astype(F32)
    x2 = r1.astype(BF16)
    x3 = (r1 - x2.astype(F32)).astype(BF16)
    return x1, x2, x3


def _by_head(x):
    lo = lax.broadcasted_iota(jnp.int32, x.shape, 1) < HEAD_DIM
    return jnp.concatenate([jnp.where(lo, x, 0.0), jnp.where(lo, 0.0, x)], axis=0).astype(BF16)


def _hgrn_gates(st, slot, cum_ref, ball_scr, k_scr):
    fl = st["f_ref"][st["rows"], st["cols"]]
    lb = st["lb_ref"][:, st["cols"]]
    e = jnp.exp(-jnp.abs(fl))
    one_plus_e = 1.0 + e
    log_sig = jnp.minimum(fl, 0.0) - jnp.log(one_plus_e)
    rcp = 1.0 / one_plus_e
    k_scr[slot, st["stream"]] = (1.0 - lb) * jnp.where(fl >= 0, e * rcp, rcp)
    log_lb = jnp.log(lb)
    cc = jnp.log(1.0 - lb) + log_sig
    g = jnp.maximum(log_lb, cc) + jnp.log(1.0 + jnp.exp(-jnp.abs(log_lb - cc)))
    cum = cum_ref[st["direction"]]
    ball_scr[slot, st["stream"]] = jnp.dot(cum, jnp.concatenate(_split3(g * LOG2_E), axis=0),
                                           preferred_element_type=F32)


def _hgrn_products(st, slot, mask_ref, sel_ref, state_ref, ball_scr, k_scr, out_scr):
    c = HGRN_CHUNK
    direction, stream = st["direction"], st["stream"]
    q, v = st["q_ref"][st["rows"], st["cols"]], st["v_ref"][st["rows"], st["cols"]]
    kk = k_scr[slot, stream]
    ball = ball_scr[slot, stream]
    b = ball[0:c]
    b_edge = b[c - 1:c] if direction == 0 else b[0:1]

    state = state_ref[stream]
    out_scr[slot, stream, 0] = _nt((q * jnp.exp2(b)).astype(BF16), state.astype(BF16))
    k_out = kk * jnp.exp2(b_edge - b)
    upd = _tn(v.astype(BF16), k_out.astype(BF16))
    r128 = lax.broadcasted_iota(jnp.int32, (LANES, LANES), 0) // HEAD_DIM
    c128 = lax.broadcasted_iota(jnp.int32, (LANES, LANES), 1) // HEAD_DIM
    state_ref[stream] = jnp.where(r128 == c128, state * jnp.exp2(b_edge) + upd, 0.0)

    att = None
    for level in range(HGRN_LEVELS):
        beta = ball[(level + 1) * c:(level + 2) * c]
        qm = mask_ref[direction, level]
        decay = jnp.exp2(-jnp.abs(b - beta))
        ql = q * decay * qm
        kl = kk * decay * (1.0 - qm)
        a = _nt(ql.astype(BF16), _by_head(kl))
        if level < HGRN_LEVELS - 1:
            a = a * mask_ref[direction, HGRN_LEVELS + level]
        att = a if att is None else att + a

    pair_cols = []
    for j in range(HGRN_SUB):
        pieces = []
        for u in range(c // HGRN_SUB):
            r = u * HGRN_SUB
            b_row = ball_scr[slot, stream, pl.ds(r + j, 1), :]
            k_row = k_scr[slot, stream, pl.ds(r + j, 1), :]
            pieces.append((q[r:r + HGRN_SUB] * k_row)
                          * jnp.exp2(jnp.minimum(b[r:r + HGRN_SUB] - b_row, 0.0)))
        pair_cols.append(jnp.concatenate(pieces, axis=0).astype(BF16))
    diag = jnp.dot(jnp.concatenate(pair_cols, axis=1), sel_ref[...], preferred_element_type=F32)
    out_scr[slot, stream, 1] = att + diag * mask_ref[direction, 2 * HGRN_LEVELS - 1]


def _hgrn_finish(st, slot, out_scr):
    v = st["v_ref"][st["rows"], st["cols"]]
    st["o_ref"][st["rows"], st["cols"]] = out_scr[slot, st["stream"], 0] + jnp.dot(
        out_scr[slot, st["stream"], 1].astype(BF16), _by_head(v), preferred_element_type=F32)


def _hgrn_body(qf_ref, vf_ref, ff_ref, qb_ref, vb_ref, fb_ref, lbf_ref, lbb_ref,
               cum_ref, mask_ref, sel_ref, of_ref, ob_ref, state_ref, ball_scr, k_scr, out_scr):
    @pl.when(pl.program_id(1) == 0)
    def _():
        state_ref[...] = jnp.zeros_like(state_ref)

    n_chunks = qf_ref.shape[0] // HGRN_CHUNK
    n_tiles = HGRN_WIDTH // LANES
    per_direction = ((qf_ref, vf_ref, ff_ref, lbf_ref, of_ref), (qb_ref, vb_ref, fb_ref, lbb_ref, ob_ref))

    def streams_of(i):
        streams = []
        for direction, (q_ref, v_ref, f_ref, lb_ref, o_ref) in enumerate(per_direction):
            chunk = i if direction == 0 else n_chunks - 1 - i
            rows = pl.ds(pl.multiple_of(chunk * HGRN_CHUNK, HGRN_CHUNK), HGRN_CHUNK)
            for tile in range(n_tiles):
                streams.append(dict(direction=direction, stream=direction * n_tiles + tile, rows=rows,
                                    cols=slice(tile * LANES, (tile + 1) * LANES), q_ref=q_ref,
                                    v_ref=v_ref, f_ref=f_ref, lb_ref=lb_ref, o_ref=o_ref))
        return streams

    for st in streams_of(0):
        _hgrn_gates(st, 0, cum_ref, ball_scr, k_scr)
    out_scr[1] = jnp.zeros(out_scr.shape[1:], out_scr.dtype)

    def step_pair(pair, carry):
        for slot in range(2):
            i = 2 * pair + slot
            for st in streams_of(jnp.maximum(i - 1, 0)):
                _hgrn_finish(st, 1 - slot, out_scr)
            for st in streams_of(jnp.minimum(i + 1, n_chunks - 1)):
                _hgrn_gates(st, 1 - slot, cum_ref, ball_scr, k_scr)
            for st in streams_of(i):
                _hgrn_products(st, slot, mask_ref, sel_ref, state_ref, ball_scr, k_scr, out_scr)
        return carry

    lax.fori_loop(0, n_chunks // 2, step_pair, 0)
    for st in streams_of(n_chunks - 1):
        _hgrn_finish(st, (n_chunks - 1) % 2, out_scr)


def hgrn_scan(rest, lb_fwd, lb_bwd, *, ts=1024):
    bsz, seq, _ = rest.shape
    ts = min(ts, seq)
    nt = seq // ts
    n_streams = 2 * HGRN_WIDTH // LANES
    cum, masks, sel = _hgrn_constants()

    def fwd(c0):
        return pl.BlockSpec((None, ts, HGRN_WIDTH), lambda b, j: (b, j, c0))

    def bwd(c0):
        return pl.BlockSpec((None, ts, HGRN_WIDTH), lambda b, j: (b, nt - 1 - j, c0))

    return pl.pallas_call(
        _hgrn_body,
        grid=(bsz, nt),
        in_specs=[fwd(1), fwd(2), fwd(3), bwd(1), bwd(2), bwd(4),
                  _resident((1, HGRN_WIDTH)), _resident((1, HGRN_WIDTH)),
                  _resident(cum.shape), _resident(masks.shape), _resident(sel.shape)],
        out_specs=[fwd(0), bwd(0)],
        out_shape=[jax.ShapeDtypeStruct((bsz, seq, HGRN_WIDTH), F32)] * 2,
        scratch_shapes=[pltpu.VMEM((n_streams, LANES, LANES), F32),
                        pltpu.VMEM((2, n_streams) + cum.shape[1:2] + (LANES,), F32),
                        pltpu.VMEM((2, n_streams, HGRN_CHUNK, LANES), F32),
                        pltpu.VMEM((2, n_streams, 2, HGRN_CHUNK, LANES), F32)],
        compiler_params=_cparams(("parallel", "arbitrary")),
    )(rest, rest, rest, rest, rest, rest, lb_fwd.reshape(1, HGRN_WIDTH), lb_bwd.reshape(1, HGRN_WIDTH),
      cum, masks, sel)


def _outproj_body(x_ref, att_ref, pool_ref, of_ref, ob_ref, gate_ref, gain_ref, bd_ref, w_ref, o_ref):
    o = of_ref[...] + ob_ref[...]
    y = o * lax.rsqrt(_head_mean_sq(o, bd_ref[...]) + EPS) * gain_ref[...]
    gate = gate_ref[...]
    rec = (y * (gate * jax.nn.sigmoid(gate))).astype(BF16)
    mixed = jnp.concatenate([att_ref[...].astype(BF16), pool_ref[...].astype(BF16), rec], axis=1)
    o_ref[...] = x_ref[...] + jnp.dot(mixed, w_ref[...], preferred_element_type=F32)


def outproj(x2, y_att, y_pool, o_fwd, o_bwd, rest, out_gain, w_out, *, tm=512):
    n, d = x2.shape
    gate_tile = (rest.shape[1] - HGRN_WIDTH) // HGRN_WIDTH

    def rowblock(width, col=0):
        return pl.BlockSpec((tm, width), lambda i: (i, col))

    return pl.pallas_call(
        _outproj_body,
        grid=(n // tm,),
        in_specs=[rowblock(d), rowblock(ATT_WIDTH), rowblock(POOL_WIDTH), rowblock(HGRN_WIDTH),
                  rowblock(HGRN_WIDTH), rowblock(HGRN_WIDTH, gate_tile),
                  _resident((1, HGRN_WIDTH)), _resident((HGRN_WIDTH, HGRN_WIDTH)),
                  _resident(w_out.shape)],
        out_specs=rowblock(d),
        out_shape=jax.ShapeDtypeStruct((n, d), F32),
        compiler_params=_cparams(("parallel",)),
    )(x2, y_att, y_pool, o_fwd, o_bwd, rest,
      jnp.tile(out_gain, HGRN_WIDTH // HEAD_DIM).reshape(1, HGRN_WIDTH), _head_blockdiag(HGRN_WIDTH),
      w_out)


def kernel(x, ffn1_norm, ffn1_w_gate, ffn1_w_up, ffn1_w_down, mix_norm, w_in, q_norm, k_norm, rel_bias,
           pool_w, pool_scale, hgrn_lb_logits, hgrn_norm, w_out, ffn2_norm, ffn2_w_gate, ffn2_w_up,
           ffn2_w_down):
    bsz, seq, d = x.shape
    depth = w_in.shape[0]
    n = bsz * seq
    h = x.astype(F32).reshape(n, d)
    bias = attention_bias(rel_bias)
    lb_cum = jnp.cumsum(jax.nn.softmax(hgrn_lb_logits.astype(F32), axis=1), axis=1)
    lb_all = lb_cum - lb_cum[:, :1]
    for l in range(depth):
        h = ffn(h, ffn1_norm[l], ffn1_w_gate[l].astype(BF16), ffn1_w_up[l].astype(BF16),
                ffn1_w_down[l].astype(BF16))
        qkv, rest = inproj(h, mix_norm[l], w_in[l].astype(BF16), q_norm[l], k_norm[l])
        qkv = qkv.reshape(bsz, seq, -1)
        rest = rest.reshape(bsz, seq, -1)
        y_att = attention(qkv, bias)
        y_pool = pool_mixer(rest, _pool_blockdiag(pool_w[l]).astype(BF16), pool_scale[l])
        o_fwd, o_bwd = hgrn_scan(rest, lb_all[0, l], lb_all[1, l])
        h = outproj(h, y_att.reshape(n, -1), y_pool.reshape(n, -1), o_fwd.reshape(n, -1),
                    o_bwd.reshape(n, -1), rest.reshape(n, -1), hgrn_norm[l], w_out[l].astype(BF16))
        h = ffn(h, ffn2_norm[l], ffn2_w_gate[l].astype(BF16), ffn2_w_up[l].astype(BF16),
                ffn2_w_down[l].astype(BF16))
    return h.reshape(bsz, seq, d).astype(x.dtype)
```

```python
import functools
import math

import numpy as np
import jax
import jax.numpy as jnp
from jax import lax
from jax.experimental import pallas as pl
from jax.experimental.pallas import tpu as pltpu

F32 = jnp.float32
BF16 = jnp.bfloat16

LANES = 128
SUBLANES = 8
MXU_COLS = 256
VMEM_BYTES_V7X = 64 * 1024 * 1024
VMEM_LIMIT = VMEM_BYTES_V7X - 8 * 1024 * 1024

HEAD_DIM = 64
HEADS_PER_TILE = LANES // HEAD_DIM
ATT_WIDTH = 512
POOL_WIDTH = 256
HGRN_WIDTH = 256
POOL_WINDOWS = (2, 4, 8, 16)
DILATIONS = (1, 4, 16)
ATT_SIDE = 64
NUM_BUCKETS = 32
MAX_DISTANCE = 1024
EPS = 1e-6
NEG = -1e30
LOG2_E = math.log2(math.e)

Q_BLOCK = 128
K_BLOCK = Q_BLOCK + 2 * ATT_SIDE
ATT_UNROLL = 4
ATT_SLABS = 4
HGRN_CHUNK = 64
HGRN_SUB = SUBLANES
HGRN_LEVELS = 3


def _cparams(sem):
    return pltpu.CompilerParams(dimension_semantics=sem, vmem_limit_bytes=VMEM_LIMIT)


def _resident(shape):
    nd = len(shape)
    return pl.BlockSpec(shape, lambda *_: (0,) * nd, pipeline_mode=pl.Buffered(1))


def _nt(a, b):
    return lax.dot_general(a, b, (((1,), (1,)), ((), ())), preferred_element_type=F32)


def _tn(a, b):
    return lax.dot_general(a, b, (((0,), (0,)), ((), ())), preferred_element_type=F32)


def _head_blockdiag(width=LANES):
    r = np.arange(width) // HEAD_DIM
    return jnp.asarray(r[:, None] == r[None, :], dtype=BF16)


def _head_mean_sq(v, bd):
    return jnp.dot((v * v).astype(BF16), bd, preferred_element_type=F32) * (1.0 / HEAD_DIM)


def _ffn_body(x_ref, g_ref, wg_ref, wu_ref, wd_ref, o_ref, a_ref, *, ff_chunk):
    x = x_ref[...]
    h = (x * lax.rsqrt(jnp.mean(x * x, axis=-1, keepdims=True) + EPS) * g_ref[...]).astype(BF16)
    for c in range(wg_ref.shape[1] // ff_chunk):
        sl = slice(c * ff_chunk, (c + 1) * ff_chunk)
        gate = jnp.dot(h, wg_ref[:, sl], preferred_element_type=F32)
        up = jnp.dot(h, wu_ref[:, sl], preferred_element_type=F32)
        a_ref[:, sl] = (gate * jax.nn.sigmoid(gate) * up).astype(BF16)
    o_ref[...] = x + 0.5 * jnp.dot(a_ref[...], wd_ref[...], preferred_element_type=F32)


def ffn(x2, gain, wg, wu, wd, *, tm=512, ff_chunk=256):
    n, d = x2.shape
    dff = wg.shape[1]
    return pl.pallas_call(
        functools.partial(_ffn_body, ff_chunk=ff_chunk),
        grid=(n // tm,),
        in_specs=[pl.BlockSpec((tm, d), lambda i: (i, 0)),
                  _resident((1, d)), _resident((d, dff)), _resident((d, dff)), _resident((dff, d))],
        out_specs=pl.BlockSpec((tm, d), lambda i: (i, 0)),
        out_shape=jax.ShapeDtypeStruct((n, d), F32),
        scratch_shapes=[pltpu.VMEM((tm, dff), BF16)],
        compiler_params=_cparams(("parallel",)),
    )(x2, gain.reshape(1, d), wg, wu, wd)


def _inproj_body(x_ref, g_ref, w_ref, qg_ref, kg_ref, bd_ref, qkv_ref, rest_ref, perm_scr):
    x = x_ref[...]
    h = (x * lax.rsqrt(jnp.mean(x * x, axis=-1, keepdims=True) + EPS) * g_ref[...]).astype(BF16)
    bd = bd_ref[...]
    wc = MXU_COLS
    slab_rows = qkv_ref.shape[1]

    def to_slabs(c, z):
        for t in range(wc // LANES):
            perm_scr[c, t] = z[:, t * LANES:(t + 1) * LANES]
            for r in range(ATT_SLABS):
                lanes = slice(c * wc + t * LANES, c * wc + (t + 1) * LANES)
                qkv_ref[r, :, lanes] = perm_scr[c, t, pl.ds(r, slab_rows, stride=ATT_SLABS), :]

    for c in range(3 * ATT_WIDTH // wc):
        sl = slice(c * wc, (c + 1) * wc)
        z = jnp.dot(h, w_ref[:, sl], preferred_element_type=F32)
        if c < 2 * ATT_WIDTH // wc:
            if c < ATT_WIDTH // wc:
                gain = qg_ref[:, sl] * (HEAD_DIM ** -0.5)
            else:
                gain = kg_ref[:, c * wc - ATT_WIDTH:(c + 1) * wc - ATT_WIDTH]
            z = z * lax.rsqrt(_head_mean_sq(z, bd) + EPS) * gain
        to_slabs(c, z)
    r0 = 3 * ATT_WIDTH
    for c in range(rest_ref.shape[1] // wc):
        rest_ref[:, c * wc:(c + 1) * wc] = jnp.dot(
            h, w_ref[:, r0 + c * wc:r0 + (c + 1) * wc], preferred_element_type=F32)


def inproj(x2, seq, gain, w_in, q_gain, k_gain, *, tm=512):
    n, d = x2.shape
    cols = w_in.shape[1]
    n_att = 3 * ATT_WIDTH
    tm = min(tm, seq)
    tiles = seq // tm
    return pl.pallas_call(
        _inproj_body,
        grid=(n // tm,),
        in_specs=[pl.BlockSpec((tm, d), lambda i: (i, 0)),
                  _resident((1, d)), _resident((d, cols)),
                  _resident((1, ATT_WIDTH)), _resident((1, ATT_WIDTH)), _resident((MXU_COLS, MXU_COLS))],
        out_specs=[pl.BlockSpec((None, ATT_SLABS, tm // ATT_SLABS, n_att),
                                lambda i: (i // tiles, 0, i % tiles, 0)),
                   pl.BlockSpec((tm, cols - n_att), lambda i: (i, 0))],
        out_shape=[jax.ShapeDtypeStruct((n // seq, ATT_SLABS, seq // ATT_SLABS, n_att), F32),
                   jax.ShapeDtypeStruct((n, cols - n_att), F32)],
        scratch_shapes=[pltpu.VMEM((n_att // MXU_COLS, MXU_COLS // LANES, tm, LANES), F32)],
        compiler_params=_cparams(("parallel",)),
    )(x2, gain.reshape(1, d), w_in, q_gain.reshape(1, ATT_WIDTH), k_gain.reshape(1, ATT_WIDTH),
      _head_blockdiag(MXU_COLS))


def _t5_bucket(rel):
    half = NUM_BUCKETS // 2
    max_exact = half // 2
    base = jnp.where(rel > 0, half, 0)
    n = jnp.abs(rel)
    nf = jnp.maximum(n, 1).astype(F32)
    large = max_exact + (jnp.log(nf / max_exact) / math.log(MAX_DISTANCE / max_exact)
                         * (half - max_exact)).astype(jnp.int32)
    large = jnp.minimum(large, half - 1)
    return base + jnp.where(n < max_exact, n, large)


def _bias_buckets():
    out = []
    for dil in DILATIONS:
        qi, ki = np.arange(Q_BLOCK), np.arange(K_BLOCK)
        if dil == 1:
            qi = qi.reshape(-1, ATT_SLABS).T.reshape(-1)
            ki = ki.reshape(-1, ATT_SLABS).T.reshape(-1)
        qi, ki = jnp.asarray(qi)[:, None], jnp.asarray(ki)[None, :]
        per_var = []
        for var in range(3):
            rel = ki - var * ATT_SIDE - qi
            per_var.append(jnp.where(jnp.abs(rel) <= ATT_SIDE, _t5_bucket(rel * dil), -1))
        out.append(jnp.stack(per_var))
    return jnp.stack(out).astype(jnp.int32)


def _bias_body(tbl_ref, bk_ref, o_ref):
    hp = pl.program_id(0)
    for var in range(3):
        bk = bk_ref[var]
        for h in range(HEADS_PER_TILE):
            head = hp * HEADS_PER_TILE + h
            acc = jnp.full(bk.shape, NEG, F32)
            for b in range(NUM_BUCKETS):
                acc = jnp.where(bk == b, tbl_ref[b, head], acc)
            o_ref[var, h] = acc


def attention_bias(rel_bias):
    n_pairs = rel_bias.shape[1] // HEADS_PER_TILE
    n_g = len(DILATIONS)
    return pl.pallas_call(
        _bias_body,
        grid=(n_pairs, n_g),
        in_specs=[pl.BlockSpec(memory_space=pltpu.SMEM),
                  pl.BlockSpec((None, 3, Q_BLOCK, K_BLOCK), lambda p, g: (g, 0, 0, 0))],
        out_specs=pl.BlockSpec((None, None, 3, HEADS_PER_TILE, Q_BLOCK, K_BLOCK),
                               lambda p, g: (p, g, 0, 0, 0, 0)),
        out_shape=jax.ShapeDtypeStruct((n_pairs, n_g, 3, HEADS_PER_TILE, Q_BLOCK, K_BLOCK), F32),
        compiler_params=_cparams(("parallel", "parallel")),
    )(rel_bias.astype(F32), _bias_buckets())


def _attn_body(q_ref, k_ref, v_ref, bias_ref, o_ref, acc_ref, m_ref, l_ref, s_scr):
    slab_len = q_ref.shape[1]
    seq = ATT_SLABS * slab_len
    lo = lax.broadcasted_iota(jnp.int32, (Q_BLOCK, LANES), 1) < HEAD_DIM
    ones = jnp.ones((K_BLOCK, LANES), BF16)

    def window(n, sub_len):
        qs = n * Q_BLOCK
        ks = jnp.clip(qs - ATT_SIDE, 0, sub_len - K_BLOCK)
        var = jnp.where(n > 0, 1, 0) + jnp.where(n == sub_len // Q_BLOCK - 1, 1, 0)
        return qs, ks, var

    def pieces(dil, it, u):
        if dil == 1:
            qs, ks, var = window(it * ATT_UNROLL + u, seq)
            q0 = pl.multiple_of(qs // ATT_SLABS, Q_BLOCK // ATT_SLABS)
            k0 = pl.multiple_of(ks // ATT_SLABS, ATT_SIDE // ATT_SLABS)
            return ([(r, pl.ds(q0, Q_BLOCK // ATT_SLABS)) for r in range(ATT_SLABS)],
                    [(r, pl.ds(k0, K_BLOCK // ATT_SLABS)) for r in range(ATT_SLABS)], var)
        if dil == ATT_SLABS:
            qs, ks, var = window(it, slab_len)
            return ([(u, pl.ds(pl.multiple_of(qs, Q_BLOCK), Q_BLOCK))],
                    [(u, pl.ds(pl.multiple_of(ks, ATT_SIDE), K_BLOCK))], var)
        step = dil // ATT_SLABS
        a, n = it % step, it // step
        qs, ks, var = window(n, slab_len // step)
        return ([(u, pl.ds(a + step * qs, Q_BLOCK, stride=step))],
                [(u, pl.ds(a + step * ks, K_BLOCK, stride=step))], var)

    def load(ref, idx):
        parts = [ref[r, rows, :] for r, rows in idx]
        return parts[0] if len(parts) == 1 else jnp.concatenate(parts, axis=0)

    def store(ref, idx, val):
        n = val.shape[0] // len(idx)
        for j, (r, rows) in enumerate(idx):
            ref[r, rows, :] = val[j * n:(j + 1) * n]

    def scores(g, q_idx, k_idx, var):
        q = load(q_ref, q_idx)
        k = load(k_ref, k_idx).astype(BF16)
        qq = jnp.concatenate([jnp.where(lo, q, 0.0), jnp.where(lo, 0.0, q)], axis=0).astype(BF16)
        return _nt(qq, k) + bias_ref[g, var]

    def weighted(k_idx, s):
        vo = jnp.concatenate([load(v_ref, k_idx).astype(BF16), ones], axis=1)
        m2 = jnp.max(s, axis=1, keepdims=True)
        ol = jnp.dot(jnp.exp(s - m2).astype(BF16), vo, preferred_element_type=F32)
        top, bot = ol[:Q_BLOCK], ol[Q_BLOCK:]
        m = jnp.where(lo, jnp.broadcast_to(m2[:Q_BLOCK], (Q_BLOCK, LANES)),
                      jnp.broadcast_to(m2[Q_BLOCK:], (Q_BLOCK, LANES)))
        return (jnp.where(lo, top[:, :LANES], bot[:, :LANES]),
                jnp.where(lo, top[:, LANES:], bot[:, LANES:]), m)

    def merge(g, q_idx, o, l, m):
        if g > 0:
            m_old = load(m_ref, q_idx)
            m_new = jnp.maximum(m_old, m)
            a_old, a_cur = jnp.exp(m_old - m_new), jnp.exp(m - m_new)
            o = load(acc_ref, q_idx) * a_old + o * a_cur
            l = load(l_ref, q_idx) * a_old + l * a_cur
            m = m_new
        store(acc_ref, q_idx, o)
        store(l_ref, q_idx, l)
        store(m_ref, q_idx, m)

    n_iter = seq // Q_BLOCK // ATT_UNROLL
    for g, dil in enumerate(DILATIONS):
        def score_into(slot, it, g=g, dil=dil):
            for u in range(ATT_UNROLL):
                q_idx, k_idx, var = pieces(dil, it, u)
                s_scr[slot, u] = scores(g, q_idx, k_idx, var)

        def step_pair(pair, carry, g=g, dil=dil, score_into=score_into):
            for slot in range(2):
                it = 2 * pair + slot
                score_into(1 - slot, jnp.minimum(it + 1, n_iter - 1))
                blocks = [pieces(dil, it, u) for u in range(ATT_UNROLL)]
                done = [weighted(k_idx, s_scr[slot, u]) for u, (_, k_idx, _) in enumerate(blocks)]
                for (q_idx, _, _), res in zip(blocks, done):
                    merge(g, q_idx, *res)
            return carry

        score_into(0, 0)
        lax.fori_loop(0, n_iter // 2, step_pair, 0)

    def normalise(i, c):
        start = pl.multiple_of(i * Q_BLOCK, Q_BLOCK)
        for r in range(ATT_SLABS):
            rows = pl.ds(start, Q_BLOCK)
            o_ref[pl.ds(r + ATT_SLABS * start, Q_BLOCK, stride=ATT_SLABS), :] = (
                acc_ref[r, rows, :] / l_ref[r, rows, :])
        return c

    lax.fori_loop(0, slab_len // Q_BLOCK, normalise, 0)


def attention(qkv, bias):
    bsz, n_slab, slab_len, _ = qkv.shape
    seq = n_slab * slab_len
    n_pairs = ATT_WIDTH // LANES
    assert n_slab == ATT_SLABS == ATT_UNROLL == DILATIONS[1] and DILATIONS[2] % ATT_SLABS == 0
    assert seq % (max(DILATIONS) * Q_BLOCK) == 0 and seq // max(DILATIONS) >= K_BLOCK
    bias = bias.reshape(bias.shape[:3] + (HEADS_PER_TILE * Q_BLOCK, K_BLOCK))
    slabs = (ATT_SLABS, slab_len, LANES)

    def col(c0):
        return pl.BlockSpec((None,) + slabs, lambda b, p: (b, 0, 0, c0 + p))

    return pl.pallas_call(
        _attn_body,
        grid=(bsz, n_pairs),
        in_specs=[col(0), col(n_pairs), col(2 * n_pairs),
                  pl.BlockSpec((None,) + bias.shape[1:], lambda b, p: (p, 0, 0, 0, 0))],
        out_specs=pl.BlockSpec((None, seq, LANES), lambda b, p: (b, 0, p)),
        out_shape=jax.ShapeDtypeStruct((bsz, seq, ATT_WIDTH), F32),
        scratch_shapes=[pltpu.VMEM(slabs, F32), pltpu.VMEM(slabs, F32), pltpu.VMEM(slabs, F32),
                        pltpu.VMEM((2, ATT_UNROLL, HEADS_PER_TILE * Q_BLOCK, K_BLOCK), F32)],
        compiler_params=_cparams(("parallel", "parallel")),
    )(qkv, qkv, qkv, bias)


POOL_HALO = max(POOL_WINDOWS) // 2


def _pool_body(u_ref, prev_ref, next_ref, w_ref, scale_ref, o_ref, pad_ref, *, seq):
    t = pl.program_id(1)
    ts = u_ref.shape[0]
    u = u_ref[...]
    pad_ref[pl.ds(0, POOL_HALO), :] = jnp.where(t > 0, prev_ref[...], 0.0)
    pad_ref[pl.ds(POOL_HALO, ts), :] = u
    pad_ref[pl.ds(POOL_HALO + ts, POOL_HALO), :] = jnp.where(t < pl.num_programs(1) - 1, next_ref[...], 0.0)

    def shifted(off):
        return pad_ref[pl.ds(POOL_HALO + off, ts), :]

    lane_win = lax.broadcasted_iota(jnp.int32, (1, POOL_WIDTH), 1) // HEAD_DIM
    pos = t * ts + lax.broadcasted_iota(jnp.int32, (ts, 1), 0)
    total = jnp.zeros_like(u)
    cnt = jnp.zeros_like(u)
    acc = None
    lo_off, hi_off = 0, 0
    for gi, win in enumerate(POOL_WINDOWS):
        for off in list(range(-(win // 2), lo_off)) + list(range(hi_off, win - win // 2)):
            acc = shifted(off) if acc is None else acc + shifted(off)
        lo_off, hi_off = -(win // 2), win - win // 2
        in_group = lane_win == gi
        n = (jnp.minimum(pos + hi_off, seq) - jnp.maximum(pos + lo_off, 0)).astype(F32)
        total = jnp.where(in_group, acc, total)
        cnt = jnp.where(in_group, n, cnt)
    mixed = jnp.dot((total / cnt - u).astype(BF16), w_ref[...], preferred_element_type=F32)
    o_ref[...] = mixed * scale_ref[...]


def pool_mixer(rest, w_blockdiag, scale, *, ts=1024):
    bsz, seq, _ = rest.shape
    ts = min(ts, seq)
    per_tile = ts // POOL_HALO
    last_halo = seq // POOL_HALO - 1
    return pl.pallas_call(
        functools.partial(_pool_body, seq=seq),
        grid=(bsz, seq // ts),
        in_specs=[pl.BlockSpec((None, ts, POOL_WIDTH), lambda b, t: (b, t, 0)),
                  pl.BlockSpec((None, POOL_HALO, POOL_WIDTH),
                               lambda b, t: (b, jnp.maximum(t * per_tile - 1, 0), 0)),
                  pl.BlockSpec((None, POOL_HALO, POOL_WIDTH),
                               lambda b, t: (b, jnp.minimum((t + 1) * per_tile, last_halo), 0)),
                  _resident((POOL_WIDTH, POOL_WIDTH)), _resident((1, POOL_WIDTH))],
        out_specs=pl.BlockSpec((None, ts, POOL_WIDTH), lambda b, t: (b, t, 0)),
        out_shape=jax.ShapeDtypeStruct((bsz, seq, POOL_WIDTH), F32),
        scratch_shapes=[pltpu.VMEM((ts + 2 * POOL_HALO, POOL_WIDTH), F32)],
        compiler_params=_cparams(("parallel", "parallel")),
    )(rest, rest, rest, w_blockdiag, scale.reshape(1, POOL_WIDTH))


def _pool_blockdiag(w_pool):
    n_g = w_pool.shape[0]
    eye = jnp.eye(n_g, dtype=w_pool.dtype)
    return jnp.einsum('gcd,gh->gchd', w_pool, eye).reshape(n_g * HEAD_DIM, n_g * HEAD_DIM)


def _hgrn_constants():
    c = HGRN_CHUNK
    t = np.arange(c)
    s_col = np.tile(t, HEADS_PER_TILE)
    cum, masks = [], []
    for direction in range(2):
        tri = (t[None, :] <= t[:, None]) if direction == 0 else (t[None, :] >= t[:, None])
        tri = tri.astype(np.float32)
        mats, qms, bms = [tri], [], []
        for level in range(HGRN_LEVELS):
            half = HGRN_SUB << level
            start = (t // (2 * half)) * (2 * half)
            boundary = start + (half - 1 if direction == 0 else half)
            mats.append(tri[boundary])
            q_right = (t // half) % 2 == 1
            qm = q_right if direction == 0 else ~q_right
            qms.append(np.broadcast_to(qm[:, None], (c, LANES)))
            bms.append(t[:, None] // (2 * half) == s_col[None, :] // (2 * half))
        same_sub = t[:, None] // HGRN_SUB == s_col[None, :] // HGRN_SUB
        causal = (s_col[None, :] <= t[:, None]) if direction == 0 else (s_col[None, :] >= t[:, None])
        cum.append(np.tile(np.concatenate(mats, axis=0), (1, 3)))
        masks.append(np.stack(qms + bms[:HGRN_LEVELS - 1] + [same_sub & causal]).astype(np.float32))
    lane_head = np.arange(LANES) // HEAD_DIM
    sel = np.concatenate([(lane_head[:, None] == lane_head[None, :]) & (s_col[None, :] % HGRN_SUB == j)
                          for j in range(HGRN_SUB)], axis=0).astype(np.float32)
    return (jnp.asarray(np.stack(cum), BF16), jnp.asarray(np.stack(masks), F32), jnp.asarray(sel, BF16))


def _split3(x):
    x1 = x.astype(BF16)
    r1 = x - x1.astype(F32)
    x2 = r1.astype(BF16)
    x3 = (r1 - x2.astype(F32)).astype(BF16)
    return x1, x2, x3


def _by_head(x):
    lo = lax.broadcasted_iota(jnp.int32, x.shape, 1) < HEAD_DIM
    return jnp.concatenate([jnp.where(lo, x, 0.0), jnp.where(lo, 0.0, x)], axis=0).astype(BF16)


def _hgrn_gates(st, slot, cum_ref, ball_scr, k_scr):
    fl = st["f_ref"][st["rows"], st["cols"]]
    lb = st["lb_ref"][:, st["cols"]]
    e = jnp.exp(-jnp.abs(fl))
    one_plus_e = 1.0 + e
    log_sig = jnp.minimum(fl, 0.0) - jnp.log(one_plus_e)
    rcp = 1.0 / one_plus_e
    k_scr[slot, st["stream"]] = (1.0 - lb) * jnp.where(fl >= 0, e * rcp, rcp)
    log_lb = jnp.log(lb)
    cc = jnp.log(1.0 - lb) + log_sig
    g = jnp.maximum(log_lb, cc) + jnp.log(1.0 + jnp.exp(-jnp.abs(log_lb - cc)))
    cum = cum_ref[st["direction"]]
    ball_scr[slot, st["stream"]] = jnp.dot(cum, jnp.concatenate(_split3(g * LOG2_E), axis=0),
                                           preferred_element_type=F32)


def _hgrn_products(st, slot, mask_ref, sel_ref, state_ref, ball_scr, k_scr, out_scr):
    c = HGRN_CHUNK
    direction, stream = st["direction"], st["stream"]
    q, v = st["q_ref"][st["rows"], st["cols"]], st["v_ref"][st["rows"], st["cols"]]
    kk = k_scr[slot, stream]
    ball = ball_scr[slot, stream]
    b = ball[0:c]
    b_edge = b[c - 1:c] if direction == 0 else b[0:1]

    state = state_ref[stream]
    out_scr[slot, stream, 0] = _nt((q * jnp.exp2(b)).astype(BF16), state.astype(BF16))
    k_out = kk * jnp.exp2(b_edge - b)
    upd = _tn(v.astype(BF16), k_out.astype(BF16))
    r128 = lax.broadcasted_iota(jnp.int32, (LANES, LANES), 0) // HEAD_DIM
    c128 = lax.broadcasted_iota(jnp.int32, (LANES, LANES), 1) // HEAD_DIM
    state_ref[stream] = jnp.where(r128 == c128, state * jnp.exp2(b_edge) + upd, 0.0)

    att = None
    for level in range(HGRN_LEVELS):
        beta = ball[(level + 1) * c:(level + 2) * c]
        qm = mask_ref[direction, level]
        decay = jnp.exp2(-jnp.abs(b - beta))
        ql = q * decay * qm
        kl = kk * decay * (1.0 - qm)
        a = _nt(ql.astype(BF16), _by_head(kl))
        if level < HGRN_LEVELS - 1:
            a = a * mask_ref[direction, HGRN_LEVELS + level]
        att = a if att is None else att + a

    pair_cols = []
    for j in range(HGRN_SUB):
        pieces = []
        for u in range(c // HGRN_SUB):
            r = u * HGRN_SUB
            b_row = ball_scr[slot, stream, pl.ds(r + j, 1), :]
            k_row = k_scr[slot, stream, pl.ds(r + j, 1), :]
            pieces.append((q[r:r + HGRN_SUB] * k_row)
                          * jnp.exp2(jnp.minimum(b[r:r + HGRN_SUB] - b_row, 0.0)))
        pair_cols.append(jnp.concatenate(pieces, axis=0).astype(BF16))
    diag = jnp.dot(jnp.concatenate(pair_cols, axis=1), sel_ref[...], preferred_element_type=F32)
    out_scr[slot, stream, 1] = att + diag * mask_ref[direction, 2 * HGRN_LEVELS - 1]


def _hgrn_finish(st, slot, out_scr):
    v = st["v_ref"][st["rows"], st["cols"]]
    st["o_ref"][st["rows"], st["cols"]] = out_scr[slot, st["stream"], 0] + jnp.dot(
        out_scr[slot, st["stream"], 1].astype(BF16), _by_head(v), preferred_element_type=F32)


def _hgrn_body(qf_ref, vf_ref, ff_ref, qb_ref, vb_ref, fb_ref, lbf_ref, lbb_ref,
               cum_ref, mask_ref, sel_ref, of_ref, ob_ref, state_ref, ball_scr, k_scr, out_scr):
    @pl.when(pl.program_id(1) == 0)
    def _():
        state_ref[...] = jnp.zeros_like(state_ref)

    n_chunks = qf_ref.shape[0] // HGRN_CHUNK
    n_tiles = HGRN_WIDTH // LANES
    per_direction = ((qf_ref, vf_ref, ff_ref, lbf_ref, of_ref), (qb_ref, vb_ref, fb_ref, lbb_ref, ob_ref))

    def streams_of(i):
        streams = []
        for direction, (q_ref, v_ref, f_ref, lb_ref, o_ref) in enumerate(per_direction):
            chunk = i if direction == 0 else n_chunks - 1 - i
            rows = pl.ds(pl.multiple_of(chunk * HGRN_CHUNK, HGRN_CHUNK), HGRN_CHUNK)
            for tile in range(n_tiles):
                streams.append(dict(direction=direction, stream=direction * n_tiles + tile, rows=rows,
                                    cols=slice(tile * LANES, (tile + 1) * LANES), q_ref=q_ref,
                                    v_ref=v_ref, f_ref=f_ref, lb_ref=lb_ref, o_ref=o_ref))
        return streams

    for st in streams_of(0):
        _hgrn_gates(st, 0, cum_ref, ball_scr, k_scr)
    out_scr[1] = jnp.zeros(out_scr.shape[1:], out_scr.dtype)

    def step_pair(pair, carry):
        for slot in range(2):
            i = 2 * pair + slot
            for st in streams_of(jnp.maximum(i - 1, 0)):
                _hgrn_finish(st, 1 - slot, out_scr)
            for st in streams_of(jnp.minimum(i + 1, n_chunks - 1)):
                _hgrn_gates(st, 1 - slot, cum_ref, ball_scr, k_scr)
            for st in streams_of(i):
                _hgrn_products(st, slot, mask_ref, sel_ref, state_ref, ball_scr, k_scr, out_scr)
        return carry

    lax.fori_loop(0, n_chunks // 2, step_pair, 0)
    for st in streams_of(n_chunks - 1):
        _hgrn_finish(st, (n_chunks - 1) % 2, out_scr)


def hgrn_scan(rest, lb_fwd, lb_bwd, *, ts=1024):
    bsz, seq, _ = rest.shape
    ts = min(ts, seq)
    nt = seq // ts
    n_streams = 2 * HGRN_WIDTH // LANES
    cum, masks, sel = _hgrn_constants()

    def fwd(c0):
        return pl.BlockSpec((None, ts, HGRN_WIDTH), lambda b, j: (b, j, c0))

    def bwd(c0):
        return pl.BlockSpec((None, ts, HGRN_WIDTH), lambda b, j: (b, nt - 1 - j, c0))

    return pl.pallas_call(
        _hgrn_body,
        grid=(bsz, nt),
        in_specs=[fwd(1), fwd(2), fwd(3), bwd(1), bwd(2), bwd(4),
                  _resident((1, HGRN_WIDTH)), _resident((1, HGRN_WIDTH)),
                  _resident(cum.shape), _resident(masks.shape), _resident(sel.shape)],
        out_specs=[fwd(0), bwd(0)],
        out_shape=[jax.ShapeDtypeStruct((bsz, seq, HGRN_WIDTH), F32)] * 2,
        scratch_shapes=[pltpu.VMEM((n_streams, LANES, LANES), F32),
                        pltpu.VMEM((2, n_streams) + cum.shape[1:2] + (LANES,), F32),
                        pltpu.VMEM((2, n_streams, HGRN_CHUNK, LANES), F32),
                        pltpu.VMEM((2, n_streams, 2, HGRN_CHUNK, LANES), F32)],
        compiler_params=_cparams(("parallel", "arbitrary")),
    )(rest, rest, rest, rest, rest, rest, lb_fwd.reshape(1, HGRN_WIDTH), lb_bwd.reshape(1, HGRN_WIDTH),
      cum, masks, sel)


def _outproj_body(x_ref, att_ref, pool_ref, of_ref, ob_ref, gate_ref, gain_ref, bd_ref, w_ref, o_ref):
    o = of_ref[...] + ob_ref[...]
    y = o * lax.rsqrt(_head_mean_sq(o, bd_ref[...]) + EPS) * gain_ref[...]
    gate = gate_ref[...]
    rec = (y * (gate * jax.nn.sigmoid(gate))).astype(BF16)
    mixed = jnp.concatenate([att_ref[...].astype(BF16), pool_ref[...].astype(BF16), rec], axis=1)
    o_ref[...] = x_ref[...] + jnp.dot(mixed, w_ref[...], preferred_element_type=F32)


def outproj(x2, y_att, y_pool, o_fwd, o_bwd, rest, out_gain, w_out, *, tm=512):
    n, d = x2.shape
    gate_tile = (rest.shape[1] - HGRN_WIDTH) // HGRN_WIDTH

    def rowblock(width, col=0):
        return pl.BlockSpec((tm, width), lambda i: (i, col))

    return pl.pallas_call(
        _outproj_body,
        grid=(n // tm,),
        in_specs=[rowblock(d), rowblock(ATT_WIDTH), rowblock(POOL_WIDTH), rowblock(HGRN_WIDTH),
                  rowblock(HGRN_WIDTH), rowblock(HGRN_WIDTH, gate_tile),
                  _resident((1, HGRN_WIDTH)), _resident((HGRN_WIDTH, HGRN_WIDTH)),
                  _resident(w_out.shape)],
        out_specs=rowblock(d),
        out_shape=jax.ShapeDtypeStruct((n, d), F32),
        compiler_params=_cparams(("parallel",)),
    )(x2, y_att, y_pool, o_fwd, o_bwd, rest,
      jnp.tile(out_gain, HGRN_WIDTH // HEAD_DIM).reshape(1, HGRN_WIDTH), _head_blockdiag(HGRN_WIDTH),
      w_out)


def kernel(x, ffn1_norm, ffn1_w_gate, ffn1_w_up, ffn1_w_down, mix_norm, w_in, q_norm, k_norm, rel_bias,
           pool_w, pool_scale, hgrn_lb_logits, hgrn_norm, w_out, ffn2_norm, ffn2_w_gate, ffn2_w_up,
           ffn2_w_down):
    bsz, seq, d = x.shape
    depth = w_in.shape[0]
    n = bsz * seq
    h = x.astype(F32).reshape(n, d)
    bias = attention_bias(rel_bias)
    lb_cum = jnp.cumsum(jax.nn.softmax(hgrn_lb_logits.astype(F32), axis=1), axis=1)
    lb_all = lb_cum - lb_cum[:, :1]
    for l in range(depth):
        h = ffn(h, ffn1_norm[l], ffn1_w_gate[l].astype(BF16), ffn1_w_up[l].astype(BF16),
                ffn1_w_down[l].astype(BF16))
        qkv, rest = inproj(h, seq, mix_norm[l], w_in[l].astype(BF16), q_norm[l], k_norm[l])
        rest = rest.reshape(bsz, seq, -1)
        y_att = attention(qkv, bias)
        y_pool = pool_mixer(rest, _pool_blockdiag(pool_w[l]).astype(BF16), pool_scale[l])
        o_fwd, o_bwd = hgrn_scan(rest, lb_all[0, l], lb_all[1, l])
        h = outproj(h, y_att.reshape(n, -1), y_pool.reshape(n, -1), o_fwd.reshape(n, -1),
                    o_bwd.reshape(n, -1), rest.reshape(n, -1), hgrn_norm[l], w_out[l].astype(BF16))
        h = ffn(h, ffn2_norm[l], ffn2_w_gate[l].astype(BF16), ffn2_w_up[l].astype(BF16),
                ffn2_w_down[l].astype(BF16))
    return h.reshape(bsz, seq, d).astype(x.dtype)
```

```python
import functools
import math

import numpy as np
import jax
import jax.numpy as jnp
from jax import lax
from jax.experimental import pallas as pl
from jax.experimental.pallas import tpu as pltpu

F32 = jnp.float32
BF16 = jnp.bfloat16

LANES = 128
SUBLANES = 8
MXU_COLS = 256
VMEM_BYTES_V7X = 64 * 1024 * 1024
VMEM_LIMIT = VMEM_BYTES_V7X - 8 * 1024 * 1024

HEAD_DIM = 64
HEADS_PER_TILE = LANES // HEAD_DIM
ATT_WIDTH = 512
POOL_WIDTH = 256
HGRN_WIDTH = 256
POOL_WINDOWS = (2, 4, 8, 16)
DILATIONS = (1, 4, 16)
ATT_SIDE = 64
NUM_BUCKETS = 32
MAX_DISTANCE = 1024
EPS = 1e-6
NEG = -1e30
LOG2_E = math.log2(math.e)

Q_BLOCK = 128
K_BLOCK = Q_BLOCK + 2 * ATT_SIDE
ATT_UNROLL = 4
ATT_SLABS = 4
HGRN_CHUNK = 64
HGRN_SUB = SUBLANES
HGRN_LEVELS = 3


def _cparams(sem):
    return pltpu.CompilerParams(dimension_semantics=sem, vmem_limit_bytes=VMEM_LIMIT)


def _resident(shape):
    nd = len(shape)
    return pl.BlockSpec(shape, lambda *_: (0,) * nd, pipeline_mode=pl.Buffered(1))


def _nt(a, b):
    return lax.dot_general(a, b, (((1,), (1,)), ((), ())), preferred_element_type=F32)


def _tn(a, b):
    return lax.dot_general(a, b, (((0,), (0,)), ((), ())), preferred_element_type=F32)


def _head_blockdiag(width=LANES):
    r = np.arange(width) // HEAD_DIM
    return jnp.asarray(r[:, None] == r[None, :], dtype=BF16)


def _head_mean_sq(v, bd):
    return jnp.dot((v * v).astype(BF16), bd, preferred_element_type=F32) * (1.0 / HEAD_DIM)


def _rmsnorm_bf16(x, gain):
    return (x * lax.rsqrt(jnp.mean(x * x, axis=-1, keepdims=True) + EPS) * gain).astype(BF16)


def _half_ffn(x, g_ref, wg_ref, wu_ref, wd_ref, a_ref):
    h = _rmsnorm_bf16(x, g_ref[...])
    for c in range(wg_ref.shape[1] // MXU_COLS):
        sl = slice(c * MXU_COLS, (c + 1) * MXU_COLS)
        gate = jnp.dot(h, wg_ref[:, sl], preferred_element_type=F32)
        up = jnp.dot(h, wu_ref[:, sl], preferred_element_type=F32)
        a_ref[:, sl] = (gate * jax.nn.sigmoid(gate) * up).astype(BF16)
    return x + 0.5 * jnp.dot(a_ref[...], wd_ref[...], preferred_element_type=F32)


def _pre_body(x_ref, g1_ref, wg_ref, wu_ref, wd_ref, g_ref, w_ref, qg_ref, kg_ref, bd_ref,
              h_ref, qkv_ref, rest_ref, a_ref, perm_scr):
    x = _half_ffn(x_ref[...], g1_ref, wg_ref, wu_ref, wd_ref, a_ref)
    h_ref[...] = x
    h = _rmsnorm_bf16(x, g_ref[...])
    bd = bd_ref[...]
    wc = MXU_COLS
    slab_rows = qkv_ref.shape[1]

    def to_slabs(c, z):
        for t in range(wc // LANES):
            perm_scr[c, t] = z[:, t * LANES:(t + 1) * LANES]
            for r in range(ATT_SLABS):
                lanes = slice(c * wc + t * LANES, c * wc + (t + 1) * LANES)
                qkv_ref[r, :, lanes] = perm_scr[c, t, pl.ds(r, slab_rows, stride=ATT_SLABS), :]

    for c in range(3 * ATT_WIDTH // wc):
        sl = slice(c * wc, (c + 1) * wc)
        z = jnp.dot(h, w_ref[:, sl], preferred_element_type=F32)
        if c < 2 * ATT_WIDTH // wc:
            if c < ATT_WIDTH // wc:
                gain = qg_ref[:, sl] * (HEAD_DIM ** -0.5)
            else:
                gain = kg_ref[:, c * wc - ATT_WIDTH:(c + 1) * wc - ATT_WIDTH]
            z = z * lax.rsqrt(_head_mean_sq(z, bd) + EPS) * gain
        to_slabs(c, z)
    r0 = 3 * ATT_WIDTH
    for c in range(rest_ref.shape[1] // wc):
        rest_ref[:, c * wc:(c + 1) * wc] = jnp.dot(
            h, w_ref[:, r0 + c * wc:r0 + (c + 1) * wc], preferred_element_type=F32)


def pre_mixer(x2, seq, ffn_gain, wg, wu, wd, gain, w_in, q_gain, k_gain, *, tm=512):
    n, d = x2.shape
    dff = wg.shape[1]
    cols = w_in.shape[1]
    n_att = 3 * ATT_WIDTH
    tm = min(tm, seq)
    tiles = seq // tm
    return pl.pallas_call(
        _pre_body,
        grid=(n // tm,),
        in_specs=[pl.BlockSpec((tm, d), lambda i: (i, 0)),
                  _resident((1, d)), _resident((d, dff)), _resident((d, dff)), _resident((dff, d)),
                  _resident((1, d)), _resident((d, cols)),
                  _resident((1, ATT_WIDTH)), _resident((1, ATT_WIDTH)), _resident((MXU_COLS, MXU_COLS))],
        out_specs=[pl.BlockSpec((tm, d), lambda i: (i, 0)),
                   pl.BlockSpec((None, ATT_SLABS, tm // ATT_SLABS, n_att),
                                lambda i: (i // tiles, 0, i % tiles, 0)),
                   pl.BlockSpec((tm, cols - n_att), lambda i: (i, 0))],
        out_shape=[jax.ShapeDtypeStruct((n, d), F32),
                   jax.ShapeDtypeStruct((n // seq, ATT_SLABS, seq // ATT_SLABS, n_att), F32),
                   jax.ShapeDtypeStruct((n, cols - n_att), F32)],
        scratch_shapes=[pltpu.VMEM((tm, dff), BF16),
                        pltpu.VMEM((n_att // MXU_COLS, MXU_COLS // LANES, tm, LANES), F32)],
        compiler_params=_cparams(("parallel",)),
    )(x2, ffn_gain.reshape(1, d), wg, wu, wd, gain.reshape(1, d), w_in,
      q_gain.reshape(1, ATT_WIDTH), k_gain.reshape(1, ATT_WIDTH), _head_blockdiag(MXU_COLS))


def _t5_bucket(rel):
    half = NUM_BUCKETS // 2
    max_exact = half // 2
    base = jnp.where(rel > 0, half, 0)
    n = jnp.abs(rel)
    nf = jnp.maximum(n, 1).astype(F32)
    large = max_exact + (jnp.log(nf / max_exact) / math.log(MAX_DISTANCE / max_exact)
                         * (half - max_exact)).astype(jnp.int32)
    large = jnp.minimum(large, half - 1)
    return base + jnp.where(n < max_exact, n, large)


def _bias_buckets():
    out = []
    for dil in DILATIONS:
        qi, ki = np.arange(Q_BLOCK), np.arange(K_BLOCK)
        if dil == 1:
            qi = qi.reshape(-1, ATT_SLABS).T.reshape(-1)
            ki = ki.reshape(-1, ATT_SLABS).T.reshape(-1)
        qi, ki = jnp.asarray(qi)[:, None], jnp.asarray(ki)[None, :]
        per_var = []
        for var in range(3):
            rel = ki - var * ATT_SIDE - qi
            per_var.append(jnp.where(jnp.abs(rel) <= ATT_SIDE, _t5_bucket(rel * dil), -1))
        out.append(jnp.stack(per_var))
    return jnp.stack(out).astype(jnp.int32)


def _bias_body(tbl_ref, bk_ref, o_ref):
    hp = pl.program_id(0)
    for var in range(3):
        bk = bk_ref[var]
        for h in range(HEADS_PER_TILE):
            head = hp * HEADS_PER_TILE + h
            acc = jnp.full(bk.shape, NEG, F32)
            for b in range(NUM_BUCKETS):
                acc = jnp.where(bk == b, tbl_ref[b, head], acc)
            o_ref[var, h] = acc


def attention_bias(rel_bias):
    n_pairs = rel_bias.shape[1] // HEADS_PER_TILE
    n_g = len(DILATIONS)
    return pl.pallas_call(
        _bias_body,
        grid=(n_pairs, n_g),
        in_specs=[pl.BlockSpec(memory_space=pltpu.SMEM),
                  pl.BlockSpec((None, 3, Q_BLOCK, K_BLOCK), lambda p, g: (g, 0, 0, 0))],
        out_specs=pl.BlockSpec((None, None, 3, HEADS_PER_TILE, Q_BLOCK, K_BLOCK),
                               lambda p, g: (p, g, 0, 0, 0, 0)),
        out_shape=jax.ShapeDtypeStruct((n_pairs, n_g, 3, HEADS_PER_TILE, Q_BLOCK, K_BLOCK), F32),
        compiler_params=_cparams(("parallel", "parallel")),
    )(rel_bias.astype(F32), _bias_buckets())


def _attn_body(q_ref, k_ref, v_ref, bias_ref, o_ref, acc_ref, m_ref, l_ref, s_scr):
    slab_len = q_ref.shape[1]
    seq = ATT_SLABS * slab_len
    lo = lax.broadcasted_iota(jnp.int32, (Q_BLOCK, LANES), 1) < HEAD_DIM
    ones = jnp.ones((K_BLOCK, LANES), BF16)

    def window(n, sub_len):
        qs = n * Q_BLOCK
        ks = jnp.clip(qs - ATT_SIDE, 0, sub_len - K_BLOCK)
        var = jnp.where(n > 0, 1, 0) + jnp.where(n == sub_len // Q_BLOCK - 1, 1, 0)
        return qs, ks, var

    def pieces(dil, it, u):
        if dil == 1:
            qs, ks, var = window(it * ATT_UNROLL + u, seq)
            q0 = pl.multiple_of(qs // ATT_SLABS, Q_BLOCK // ATT_SLABS)
            k0 = pl.multiple_of(ks // ATT_SLABS, ATT_SIDE // ATT_SLABS)
            return ([(r, pl.ds(q0, Q_BLOCK // ATT_SLABS)) for r in range(ATT_SLABS)],
                    [(r, pl.ds(k0, K_BLOCK // ATT_SLABS)) for r in range(ATT_SLABS)], var)
        if dil == ATT_SLABS:
            qs, ks, var = window(it, slab_len)
            return ([(u, pl.ds(pl.multiple_of(qs, Q_BLOCK), Q_BLOCK))],
                    [(u, pl.ds(pl.multiple_of(ks, ATT_SIDE), K_BLOCK))], var)
        step = dil // ATT_SLABS
        a, n = it % step, it // step
        qs, ks, var = window(n, slab_len // step)
        return ([(u, pl.ds(a + step * qs, Q_BLOCK, stride=step))],
                [(u, pl.ds(a + step * ks, K_BLOCK, stride=step))], var)

    def load(ref, idx):
        parts = [ref[r, rows, :] for r, rows in idx]
        return parts[0] if len(parts) == 1 else jnp.concatenate(parts, axis=0)

    def store(ref, idx, val):
        n = val.shape[0] // len(idx)
        for j, (r, rows) in enumerate(idx):
            ref[r, rows, :] = val[j * n:(j + 1) * n]

    def scores(g, q_idx, k_idx, var):
        q = load(q_ref, q_idx)
        k = load(k_ref, k_idx).astype(BF16)
        qq = jnp.concatenate([jnp.where(lo, q, 0.0), jnp.where(lo, 0.0, q)], axis=0).astype(BF16)
        return _nt(qq, k) + bias_ref[g, var]

    def weighted(k_idx, s):
        vo = jnp.concatenate([load(v_ref, k_idx).astype(BF16), ones], axis=1)
        m2 = jnp.max(s, axis=1, keepdims=True)
        ol = jnp.dot(jnp.exp(s - m2).astype(BF16), vo, preferred_element_type=F32)
        top, bot = ol[:Q_BLOCK], ol[Q_BLOCK:]
        m = jnp.where(lo, jnp.broadcast_to(m2[:Q_BLOCK], (Q_BLOCK, LANES)),
                      jnp.broadcast_to(m2[Q_BLOCK:], (Q_BLOCK, LANES)))
        return (jnp.where(lo, top[:, :LANES], bot[:, :LANES]),
                jnp.where(lo, top[:, LANES:], bot[:, LANES:]), m)

    def merge(g, q_idx, o, l, m):
        if g > 0:
            m_old = load(m_ref, q_idx)
            m_new = jnp.maximum(m_old, m)
            a_old, a_cur = jnp.exp(m_old - m_new), jnp.exp(m - m_new)
            o = load(acc_ref, q_idx) * a_old + o * a_cur
            l = load(l_ref, q_idx) * a_old + l * a_cur
            m = m_new
        store(acc_ref, q_idx, o)
        store(l_ref, q_idx, l)
        store(m_ref, q_idx, m)

    n_iter = seq // Q_BLOCK // ATT_UNROLL
    for g, dil in enumerate(DILATIONS):
        def score_into(slot, it, g=g, dil=dil):
            for u in range(ATT_UNROLL):
                q_idx, k_idx, var = pieces(dil, it, u)
                s_scr[slot, u] = scores(g, q_idx, k_idx, var)

        def step_pair(pair, carry, g=g, dil=dil, score_into=score_into):
            for slot in range(2):
                it = 2 * pair + slot
                score_into(1 - slot, jnp.minimum(it + 1, n_iter - 1))
                blocks = [pieces(dil, it, u) for u in range(ATT_UNROLL)]
                done = [weighted(k_idx, s_scr[slot, u]) for u, (_, k_idx, _) in enumerate(blocks)]
                for (q_idx, _, _), res in zip(blocks, done):
                    merge(g, q_idx, *res)
            return carry

        score_into(0, 0)
        lax.fori_loop(0, n_iter // 2, step_pair, 0)

    def normalise(i, c):
        start = pl.multiple_of(i * Q_BLOCK, Q_BLOCK)
        for r in range(ATT_SLABS):
            rows = pl.ds(start, Q_BLOCK)
            o_ref[pl.ds(r + ATT_SLABS * start, Q_BLOCK, stride=ATT_SLABS), :] = (
                acc_ref[r, rows, :] / l_ref[r, rows, :])
        return c

    lax.fori_loop(0, slab_len // Q_BLOCK, normalise, 0)


def attention(qkv, bias):
    bsz, n_slab, slab_len, _ = qkv.shape
    seq = n_slab * slab_len
    n_pairs = ATT_WIDTH // LANES
    assert n_slab == ATT_SLABS == ATT_UNROLL == DILATIONS[1] and DILATIONS[2] % ATT_SLABS == 0
    assert seq % (max(DILATIONS) * Q_BLOCK) == 0 and seq // max(DILATIONS) >= K_BLOCK
    bias = bias.reshape(bias.shape[:3] + (HEADS_PER_TILE * Q_BLOCK, K_BLOCK))
    slabs = (ATT_SLABS, slab_len, LANES)

    def col(c0):
        return pl.BlockSpec((None,) + slabs, lambda b, p: (b, 0, 0, c0 + p))

    return pl.pallas_call(
        _attn_body,
        grid=(bsz, n_pairs),
        in_specs=[col(0), col(n_pairs), col(2 * n_pairs),
                  pl.BlockSpec((None,) + bias.shape[1:], lambda b, p: (p, 0, 0, 0, 0))],
        out_specs=pl.BlockSpec((None, seq, LANES), lambda b, p: (b, 0, p)),
        out_shape=jax.ShapeDtypeStruct((bsz, seq, ATT_WIDTH), F32),
        scratch_shapes=[pltpu.VMEM(slabs, F32), pltpu.VMEM(slabs, F32), pltpu.VMEM(slabs, F32),
                        pltpu.VMEM((2, ATT_UNROLL, HEADS_PER_TILE * Q_BLOCK, K_BLOCK), F32)],
        compiler_params=_cparams(("parallel", "parallel")),
    )(qkv, qkv, qkv, bias)


POOL_HALO = max(POOL_WINDOWS) // 2


def _pooled(t, n_tiles, u_ref, prev_ref, next_ref, w_ref, scale_ref, pad_ref, *, seq):
    ts = u_ref.shape[0]
    u = u_ref[...]
    pad_ref[pl.ds(0, POOL_HALO), :] = jnp.where(t > 0, prev_ref[...], 0.0)
    pad_ref[pl.ds(POOL_HALO, ts), :] = u
    pad_ref[pl.ds(POOL_HALO + ts, POOL_HALO), :] = jnp.where(t < n_tiles - 1, next_ref[...], 0.0)

    def shifted(off):
        return pad_ref[pl.ds(POOL_HALO + off, ts), :]

    lane_win = lax.broadcasted_iota(jnp.int32, (1, POOL_WIDTH), 1) // HEAD_DIM
    pos = t * ts + lax.broadcasted_iota(jnp.int32, (ts, 1), 0)
    total = jnp.zeros_like(u)
    cnt = jnp.zeros_like(u)
    acc = None
    lo_off, hi_off = 0, 0
    for gi, win in enumerate(POOL_WINDOWS):
        for off in list(range(-(win // 2), lo_off)) + list(range(hi_off, win - win // 2)):
            acc = shifted(off) if acc is None else acc + shifted(off)
        lo_off, hi_off = -(win // 2), win - win // 2
        in_group = lane_win == gi
        n = (jnp.minimum(pos + hi_off, seq) - jnp.maximum(pos + lo_off, 0)).astype(F32)
        total = jnp.where(in_group, acc, total)
        cnt = jnp.where(in_group, n, cnt)
    mixed = jnp.dot((total / cnt - u).astype(BF16), w_ref[...], preferred_element_type=F32)
    return mixed * scale_ref[...]


def _pool_blockdiag(w_pool):
    n_g = w_pool.shape[0]
    eye = jnp.eye(n_g, dtype=w_pool.dtype)
    return jnp.einsum('gcd,gh->gchd', w_pool, eye).reshape(n_g * HEAD_DIM, n_g * HEAD_DIM)


def _hgrn_constants():
    c = HGRN_CHUNK
    t = np.arange(c)
    s_col = np.tile(t, HEADS_PER_TILE)
    cum, masks = [], []
    for direction in range(2):
        tri = (t[None, :] <= t[:, None]) if direction == 0 else (t[None, :] >= t[:, None])
        tri = tri.astype(np.float32)
        mats, qms, bms = [tri], [], []
        for level in range(HGRN_LEVELS):
            half = HGRN_SUB << level
            start = (t // (2 * half)) * (2 * half)
            boundary = start + (half - 1 if direction == 0 else half)
            mats.append(tri[boundary])
            q_right = (t // half) % 2 == 1
            qm = q_right if direction == 0 else ~q_right
            qms.append(np.broadcast_to(qm[:, None], (c, LANES)))
            bms.append(t[:, None] // (2 * half) == s_col[None, :] // (2 * half))
        same_sub = t[:, None] // HGRN_SUB == s_col[None, :] // HGRN_SUB
        causal = (s_col[None, :] <= t[:, None]) if direction == 0 else (s_col[None, :] >= t[:, None])
        cum.append(np.tile(np.concatenate(mats, axis=0), (1, 3)))
        masks.append(np.stack(qms + bms[:HGRN_LEVELS - 1] + [same_sub & causal]).astype(np.float32))
    lane_head = np.arange(LANES) // HEAD_DIM
    sel = np.concatenate([(lane_head[:, None] == lane_head[None, :]) & (s_col[None, :] % HGRN_SUB == j)
                          for j in range(HGRN_SUB)], axis=0).astype(np.float32)
    return (jnp.asarray(np.stack(cum), BF16), jnp.asarray(np.stack(masks), F32), jnp.asarray(sel, BF16))


def _split3(x):
    x1 = x.astype(BF16)
    r1 = x - x1.astype(F32)
    x2 = r1.astype(BF16)
    x3 = (r1 - x2.astype(F32)).astype(BF16)
    return x1, x2, x3


def _by_head(x):
    lo = lax.broadcasted_iota(jnp.int32, x.shape, 1) < HEAD_DIM
    return jnp.concatenate([jnp.where(lo, x, 0.0), jnp.where(lo, 0.0, x)], axis=0).astype(BF16)


def _hgrn_gates(st, slot, cum_ref, ball_scr, k_scr):
    fl = st["f_ref"][st["rows"], st["cols"]]
    lb = st["lb_ref"][:, st["cols"]]
    e = jnp.exp(-jnp.abs(fl))
    one_plus_e = 1.0 + e
    log_sig = jnp.minimum(fl, 0.0) - jnp.log(one_plus_e)
    rcp = 1.0 / one_plus_e
    k_scr[slot, st["stream"]] = (1.0 - lb) * jnp.where(fl >= 0, e * rcp, rcp)
    log_lb = jnp.log(lb)
    cc = jnp.log(1.0 - lb) + log_sig
    g = jnp.maximum(log_lb, cc) + jnp.log(1.0 + jnp.exp(-jnp.abs(log_lb - cc)))
    cum = cum_ref[st["direction"]]
    ball_scr[slot, st["stream"]] = jnp.dot(cum, jnp.concatenate(_split3(g * LOG2_E), axis=0),
                                           preferred_element_type=F32)


def _hgrn_products(st, slot, mask_ref, sel_ref, state_ref, ball_scr, k_scr, out_scr):
    c = HGRN_CHUNK
    direction, stream = st["direction"], st["stream"]
    q, v = st["q_ref"][st["rows"], st["cols"]], st["v_ref"][st["rows"], st["cols"]]
    kk = k_scr[slot, stream]
    ball = ball_scr[slot, stream]
    b = ball[0:c]
    b_edge = b[c - 1:c] if direction == 0 else b[0:1]

    state = state_ref[stream]
    out_scr[slot, stream, 0] = _nt((q * jnp.exp2(b)).astype(BF16), state.astype(BF16))
    k_out = kk * jnp.exp2(b_edge - b)
    upd = _tn(v.astype(BF16), k_out.astype(BF16))
    r128 = lax.broadcasted_iota(jnp.int32, (LANES, LANES), 0) // HEAD_DIM
    c128 = lax.broadcasted_iota(jnp.int32, (LANES, LANES), 1) // HEAD_DIM
    state_ref[stream] = jnp.where(r128 == c128, state * jnp.exp2(b_edge) + upd, 0.0)

    att = None
    for level in range(HGRN_LEVELS):
        beta = ball[(level + 1) * c:(level + 2) * c]
        qm = mask_ref[direction, level]
        decay = jnp.exp2(-jnp.abs(b - beta))
        ql = q * decay * qm
        kl = kk * decay * (1.0 - qm)
        a = _nt(ql.astype(BF16), _by_head(kl))
        if level < HGRN_LEVELS - 1:
            a = a * mask_ref[direction, HGRN_LEVELS + level]
        att = a if att is None else att + a

    pair_cols = []
    for j in range(HGRN_SUB):
        pieces = []
        for u in range(c // HGRN_SUB):
            r = u * HGRN_SUB
            b_row = ball_scr[slot, stream, pl.ds(r + j, 1), :]
            k_row = k_scr[slot, stream, pl.ds(r + j, 1), :]
            pieces.append((q[r:r + HGRN_SUB] * k_row)
                          * jnp.exp2(jnp.minimum(b[r:r + HGRN_SUB] - b_row, 0.0)))
        pair_cols.append(jnp.concatenate(pieces, axis=0).astype(BF16))
    diag = jnp.dot(jnp.concatenate(pair_cols, axis=1), sel_ref[...], preferred_element_type=F32)
    out_scr[slot, stream, 1] = att + diag * mask_ref[direction, 2 * HGRN_LEVELS - 1]


def _hgrn_finish(st, slot, out_scr):
    v = st["v_ref"][st["rows"], st["cols"]]
    st["o_ref"][st["rows"], st["cols"]] = out_scr[slot, st["stream"], 0] + jnp.dot(
        out_scr[slot, st["stream"], 1].astype(BF16), _by_head(v), preferred_element_type=F32)


def _hgrn_body(qf_ref, vf_ref, ff_ref, qb_ref, vb_ref, fb_ref, lbf_ref, lbb_ref,
               cum_ref, mask_ref, sel_ref, of_ref, ob_ref, state_ref, ball_scr, k_scr, out_scr):
    @pl.when(pl.program_id(1) == 0)
    def _():
        state_ref[...] = jnp.zeros_like(state_ref)

    n_chunks = qf_ref.shape[0] // HGRN_CHUNK
    n_tiles = HGRN_WIDTH // LANES
    per_direction = ((qf_ref, vf_ref, ff_ref, lbf_ref, of_ref), (qb_ref, vb_ref, fb_ref, lbb_ref, ob_ref))

    def streams_of(i):
        streams = []
        for direction, (q_ref, v_ref, f_ref, lb_ref, o_ref) in enumerate(per_direction):
            chunk = i if direction == 0 else n_chunks - 1 - i
            rows = pl.ds(pl.multiple_of(chunk * HGRN_CHUNK, HGRN_CHUNK), HGRN_CHUNK)
            for tile in range(n_tiles):
                streams.append(dict(direction=direction, stream=direction * n_tiles + tile, rows=rows,
                                    cols=slice(tile * LANES, (tile + 1) * LANES), q_ref=q_ref,
                                    v_ref=v_ref, f_ref=f_ref, lb_ref=lb_ref, o_ref=o_ref))
        return streams

    for st in streams_of(0):
        _hgrn_gates(st, 0, cum_ref, ball_scr, k_scr)
    out_scr[1] = jnp.zeros(out_scr.shape[1:], out_scr.dtype)

    def step_pair(pair, carry):
        for slot in range(2):
            i = 2 * pair + slot
            for st in streams_of(jnp.maximum(i - 1, 0)):
                _hgrn_finish(st, 1 - slot, out_scr)
            for st in streams_of(jnp.minimum(i + 1, n_chunks - 1)):
                _hgrn_gates(st, 1 - slot, cum_ref, ball_scr, k_scr)
            for st in streams_of(i):
                _hgrn_products(st, slot, mask_ref, sel_ref, state_ref, ball_scr, k_scr, out_scr)
        return carry

    lax.fori_loop(0, n_chunks // 2, step_pair, 0)
    for st in streams_of(n_chunks - 1):
        _hgrn_finish(st, (n_chunks - 1) % 2, out_scr)


def hgrn_scan(rest, lb_fwd, lb_bwd, *, ts=1024):
    bsz, seq, _ = rest.shape
    ts = min(ts, seq)
    nt = seq // ts
    n_streams = 2 * HGRN_WIDTH // LANES
    cum, masks, sel = _hgrn_constants()

    def fwd(c0):
        return pl.BlockSpec((None, ts, HGRN_WIDTH), lambda b, j: (b, j, c0))

    def bwd(c0):
        return pl.BlockSpec((None, ts, HGRN_WIDTH), lambda b, j: (b, nt - 1 - j, c0))

    return pl.pallas_call(
        _hgrn_body,
        grid=(bsz, nt),
        in_specs=[fwd(1), fwd(2), fwd(3), bwd(1), bwd(2), bwd(4),
                  _resident((1, HGRN_WIDTH)), _resident((1, HGRN_WIDTH)),
                  _resident(cum.shape), _resident(masks.shape), _resident(sel.shape)],
        out_specs=[fwd(0), bwd(0)],
        out_shape=[jax.ShapeDtypeStruct((bsz, seq, HGRN_WIDTH), F32)] * 2,
        scratch_shapes=[pltpu.VMEM((n_streams, LANES, LANES), F32),
                        pltpu.VMEM((2, n_streams) + cum.shape[1:2] + (LANES,), F32),
                        pltpu.VMEM((2, n_streams, HGRN_CHUNK, LANES), F32),
                        pltpu.VMEM((2, n_streams, 2, HGRN_CHUNK, LANES), F32)],
        compiler_params=_cparams(("parallel", "arbitrary")),
    )(rest, rest, rest, rest, rest, rest, lb_fwd.reshape(1, HGRN_WIDTH), lb_bwd.reshape(1, HGRN_WIDTH),
      cum, masks, sel)


def _post_body(x_ref, att_ref, u_ref, prev_ref, next_ref, of_ref, ob_ref, gate_ref,
               pw_ref, ps_ref, gain_ref, bd_ref, w_ref, g2_ref, wg_ref, wu_ref, wd_ref,
               o_ref, pad_ref, a_ref, *, seq):
    pooled = _pooled(pl.program_id(1), pl.num_programs(1), u_ref, prev_ref, next_ref, pw_ref, ps_ref,
                     pad_ref, seq=seq)
    o = of_ref[...] + ob_ref[...]
    y = o * lax.rsqrt(_head_mean_sq(o, bd_ref[...]) + EPS) * gain_ref[...]
    gate = gate_ref[...]
    rec = (y * (gate * jax.nn.sigmoid(gate))).astype(BF16)
    mixed = jnp.concatenate([att_ref[...].astype(BF16), pooled.astype(BF16), rec], axis=1)
    x = x_ref[...] + jnp.dot(mixed, w_ref[...], preferred_element_type=F32)
    o_ref[...] = _half_ffn(x, g2_ref, wg_ref, wu_ref, wd_ref, a_ref)


def post_mixer(x3, y_att, rest, o_fwd, o_bwd, pool_w, pool_scale, out_gain, w_out,
               ffn_gain, wg, wu, wd, *, tm=512):
    bsz, seq, d = x3.shape
    dff = wg.shape[1]
    tm = min(tm, seq)
    gate_tile = (rest.shape[2] - HGRN_WIDTH) // HGRN_WIDTH
    per_tile = tm // POOL_HALO
    last_halo = seq // POOL_HALO - 1

    def rowblock(width, col=0):
        return pl.BlockSpec((None, tm, width), lambda b, t: (b, t, col))

    return pl.pallas_call(
        functools.partial(_post_body, seq=seq),
        grid=(bsz, seq // tm),
        in_specs=[rowblock(d), rowblock(ATT_WIDTH), rowblock(POOL_WIDTH),
                  pl.BlockSpec((None, POOL_HALO, POOL_WIDTH),
                               lambda b, t: (b, jnp.maximum(t * per_tile - 1, 0), 0)),
                  pl.BlockSpec((None, POOL_HALO, POOL_WIDTH),
                               lambda b, t: (b, jnp.minimum((t + 1) * per_tile, last_halo), 0)),
                  rowblock(HGRN_WIDTH), rowblock(HGRN_WIDTH), rowblock(HGRN_WIDTH, gate_tile),
                  _resident((POOL_WIDTH, POOL_WIDTH)), _resident((1, POOL_WIDTH)),
                  _resident((1, HGRN_WIDTH)), _resident((HGRN_WIDTH, HGRN_WIDTH)), _resident(w_out.shape),
                  _resident((1, d)), _resident((d, dff)), _resident((d, dff)), _resident((dff, d))],
        out_specs=rowblock(d),
        out_shape=jax.ShapeDtypeStruct((bsz, seq, d), F32),
        scratch_shapes=[pltpu.VMEM((tm + 2 * POOL_HALO, POOL_WIDTH), F32), pltpu.VMEM((tm, dff), BF16)],
        compiler_params=_cparams(("parallel", "parallel")),
    )(x3, y_att, rest, rest, rest, o_fwd, o_bwd, rest,
      _pool_blockdiag(pool_w).astype(BF16), pool_scale.reshape(1, POOL_WIDTH),
      jnp.tile(out_gain, HGRN_WIDTH // HEAD_DIM).reshape(1, HGRN_WIDTH), _head_blockdiag(HGRN_WIDTH),
      w_out, ffn_gain.reshape(1, d), wg, wu, wd)


def kernel(x, ffn1_norm, ffn1_w_gate, ffn1_w_up, ffn1_w_down, mix_norm, w_in, q_norm, k_norm, rel_bias,
           pool_w, pool_scale, hgrn_lb_logits, hgrn_norm, w_out, ffn2_norm, ffn2_w_gate, ffn2_w_up,
           ffn2_w_down):
    bsz, seq, d = x.shape
    depth = w_in.shape[0]
    n = bsz * seq
    h = x.astype(F32).reshape(n, d)
    bias = attention_bias(rel_bias)
    lb_cum = jnp.cumsum(jax.nn.softmax(hgrn_lb_logits.astype(F32), axis=1), axis=1)
    lb_all = lb_cum - lb_cum[:, :1]
    for l in range(depth):
        h, qkv, rest = pre_mixer(h, seq, ffn1_norm[l], ffn1_w_gate[l].astype(BF16),
                                 ffn1_w_up[l].astype(BF16), ffn1_w_down[l].astype(BF16),
                                 mix_norm[l], w_in[l].astype(BF16), q_norm[l], k_norm[l])
        rest = rest.reshape(bsz, seq, -1)
        y_att = attention(qkv, bias)
        o_fwd, o_bwd = hgrn_scan(rest, lb_all[0, l], lb_all[1, l])
        h = post_mixer(h.reshape(bsz, seq, d), y_att, rest, o_fwd, o_bwd, pool_w[l], pool_scale[l],
                       hgrn_norm[l], w_out[l].astype(BF16), ffn2_norm[l], ffn2_w_gate[l].astype(BF16),
                       ffn2_w_up[l].astype(BF16), ffn2_w_down[l].astype(BF16)).reshape(n, d)
    return h.reshape(bsz, seq, d).astype(x.dtype)
```

```python
import functools
import math

import numpy as np
import jax
import jax.numpy as jnp
from jax import lax
from jax.experimental import pallas as pl
from jax.experimental.pallas import tpu as pltpu

F32 = jnp.float32
BF16 = jnp.bfloat16

LANES = 128
SUBLANES = 8
MXU_COLS = 256
VMEM_BYTES_V7X = 64 * 1024 * 1024
VMEM_LIMIT = VMEM_BYTES_V7X - 8 * 1024 * 1024

HEAD_DIM = 64
HEADS_PER_TILE = LANES // HEAD_DIM
ATT_WIDTH = 512
POOL_WIDTH = 256
HGRN_WIDTH = 256
POOL_WINDOWS = (2, 4, 8, 16)
DILATIONS = (1, 4, 16)
ATT_SIDE = 64
NUM_BUCKETS = 32
MAX_DISTANCE = 1024
EPS = 1e-6
NEG = -1e30
LOG2_E = math.log2(math.e)

Q_BLOCK = 128
K_BLOCK = Q_BLOCK + 2 * ATT_SIDE
ATT_UNROLL = 4
ATT_SLABS = 4
HGRN_CHUNK = 64
HGRN_SUB = SUBLANES
HGRN_LEVELS = 3
HGRN_UNROLL = 4


def _cparams(sem):
    return pltpu.CompilerParams(dimension_semantics=sem, vmem_limit_bytes=VMEM_LIMIT)


def _resident(shape):
    nd = len(shape)
    return pl.BlockSpec(shape, lambda *_: (0,) * nd, pipeline_mode=pl.Buffered(1))


def _nt(a, b):
    return lax.dot_general(a, b, (((1,), (1,)), ((), ())), preferred_element_type=F32)


def _tn(a, b):
    return lax.dot_general(a, b, (((0,), (0,)), ((), ())), preferred_element_type=F32)


def _head_blockdiag(width=LANES):
    r = np.arange(width) // HEAD_DIM
    return jnp.asarray(r[:, None] == r[None, :], dtype=BF16)


def _head_mean_sq(v, bd):
    return jnp.dot((v * v).astype(BF16), bd, preferred_element_type=F32) * (1.0 / HEAD_DIM)


def _rmsnorm_bf16(x, gain):
    return (x * lax.rsqrt(jnp.mean(x * x, axis=-1, keepdims=True) + EPS) * gain).astype(BF16)


def _half_ffn(x, g_ref, wg_ref, wu_ref, wd_ref, a_ref, between=()):
    h = _rmsnorm_bf16(x, g_ref[...])
    between = list(between)
    for c in range(wg_ref.shape[1] // MXU_COLS):
        sl = slice(c * MXU_COLS, (c + 1) * MXU_COLS)
        gate = jnp.dot(h, wg_ref[:, sl], preferred_element_type=F32)
        up = jnp.dot(h, wu_ref[:, sl], preferred_element_type=F32)
        a_ref[:, sl] = (gate * jax.nn.sigmoid(gate) * up).astype(BF16)
        thunk = between.pop(0) if between else None
        if thunk is not None:
            thunk()
    for thunk in between:
        thunk()
    return x + 0.5 * jnp.dot(a_ref[...], wd_ref[...], preferred_element_type=F32)


def _pre_body(x_ref, g1_ref, wg_ref, wu_ref, wd_ref, g_ref, w_ref, qg_ref, kg_ref, bd_ref,
              h_ref, qkv_ref, rest_ref, a_ref, perm_scr, hn_scr):
    @pl.when(pl.program_id(0) == 0)
    def _():
        hn_scr[...] = jnp.zeros_like(hn_scr)

    bd = bd_ref[...]
    wc = MXU_COLS
    slab_rows = qkv_ref.shape[1]
    n_att = 3 * ATT_WIDTH

    def to_slabs(c, z):
        for t in range(wc // LANES):
            perm_scr[c, t] = z[:, t * LANES:(t + 1) * LANES]
            for r in range(ATT_SLABS):
                lanes = slice(c * wc + t * LANES, c * wc + (t + 1) * LANES)
                qkv_ref[r, :, lanes] = perm_scr[c, t, pl.ds(r, slab_rows, stride=ATT_SLABS), :]

    projected = {}

    def project(c):
        projected[c] = jnp.dot(hn_scr[...], w_ref[:, c * wc:(c + 1) * wc], preferred_element_type=F32)

    def finish(c):
        sl = slice(c * wc, (c + 1) * wc)
        z = projected.pop(c)
        if c >= n_att // wc:
            rest_ref[:, c * wc - n_att:(c + 1) * wc - n_att] = z
            return
        if c < 2 * ATT_WIDTH // wc:
            if c < ATT_WIDTH // wc:
                gain = qg_ref[:, sl] * (HEAD_DIM ** -0.5)
            else:
                gain = kg_ref[:, c * wc - ATT_WIDTH:(c + 1) * wc - ATT_WIDTH]
            z = z * lax.rsqrt(_head_mean_sq(z, bd) + EPS) * gain
        to_slabs(c, z)

    def skewed(c):
        def thunk():
            if c > 0:
                finish(c - 1)
            if c < n_chunks:
                project(c)
        return thunk

    n_chunks = w_ref.shape[1] // wc
    thunks = [skewed(c) for c in range(n_chunks + 1)]
    lead = 2
    for thunk in thunks[:lead]:
        thunk()
    x = _half_ffn(x_ref[...], g1_ref, wg_ref, wu_ref, wd_ref, a_ref, between=thunks[lead:])
    h_ref[...] = x
    hn_scr[...] = _rmsnorm_bf16(x, g_ref[...])


def pre_mixer(x2, seq, ffn_gain, wg, wu, wd, gain, w_in, q_gain, k_gain, *, tm=512):
    n, d = x2.shape
    dff = wg.shape[1]
    cols = w_in.shape[1]
    n_att = 3 * ATT_WIDTH
    tm = min(tm, seq)
    tiles = seq // tm
    last = n // tm - 1

    def ffn_tile(i):
        return jnp.minimum(i, last)

    def proj_tile(i):
        return jnp.maximum(i - 1, 0)

    return pl.pallas_call(
        _pre_body,
        grid=(n // tm + 1,),
        in_specs=[pl.BlockSpec((tm, d), lambda i: (ffn_tile(i), 0)),
                  _resident((1, d)), _resident((d, dff)), _resident((d, dff)), _resident((dff, d)),
                  _resident((1, d)), _resident((d, cols)),
                  _resident((1, ATT_WIDTH)), _resident((1, ATT_WIDTH)), _resident((MXU_COLS, MXU_COLS))],
        out_specs=[pl.BlockSpec((tm, d), lambda i: (ffn_tile(i), 0)),
                   pl.BlockSpec((None, ATT_SLABS, tm // ATT_SLABS, n_att),
                                lambda i: (proj_tile(i) // tiles, 0, proj_tile(i) % tiles, 0)),
                   pl.BlockSpec((tm, cols - n_att), lambda i: (proj_tile(i), 0))],
        out_shape=[jax.ShapeDtypeStruct((n, d), F32),
                   jax.ShapeDtypeStruct((n // seq, ATT_SLABS, seq // ATT_SLABS, n_att), F32),
                   jax.ShapeDtypeStruct((n, cols - n_att), F32)],
        scratch_shapes=[pltpu.VMEM((tm, dff), BF16),
                        pltpu.VMEM((n_att // MXU_COLS, MXU_COLS // LANES, tm, LANES), F32),
                        pltpu.VMEM((tm, d), BF16)],
        compiler_params=_cparams(("arbitrary",)),
    )(x2, ffn_gain.reshape(1, d), wg, wu, wd, gain.reshape(1, d), w_in,
      q_gain.reshape(1, ATT_WIDTH), k_gain.reshape(1, ATT_WIDTH), _head_blockdiag(MXU_COLS))


def _t5_bucket(rel):
    half = NUM_BUCKETS // 2
    max_exact = half // 2
    base = jnp.where(rel > 0, half, 0)
    n = jnp.abs(rel)
    nf = jnp.maximum(n, 1).astype(F32)
    large = max_exact + (jnp.log(nf / max_exact) / math.log(MAX_DISTANCE / max_exact)
                         * (half - max_exact)).astype(jnp.int32)
    large = jnp.minimum(large, half - 1)
    return base + jnp.where(n < max_exact, n, large)


def _bias_buckets():
    out = []
    for dil in DILATIONS:
        qi, ki = np.arange(Q_BLOCK), np.arange(K_BLOCK)
        if dil == 1:
            qi = qi.reshape(-1, ATT_SLABS).T.reshape(-1)
            ki = ki.reshape(-1, ATT_SLABS).T.reshape(-1)
        qi, ki = jnp.asarray(qi)[:, None], jnp.asarray(ki)[None, :]
        per_var = []
        for var in range(3):
            rel = ki - var * ATT_SIDE - qi
            per_var.append(jnp.where(jnp.abs(rel) <= ATT_SIDE, _t5_bucket(rel * dil), -1))
        out.append(jnp.stack(per_var))
    return jnp.stack(out).astype(jnp.int32)


def _bias_body(tbl_ref, bk_ref, o_ref):
    hp = pl.program_id(0)
    for var in range(3):
        bk = bk_ref[var]
        for h in range(HEADS_PER_TILE):
            head = hp * HEADS_PER_TILE + h
            acc = jnp.full(bk.shape, NEG, F32)
            for b in range(NUM_BUCKETS):
                acc = jnp.where(bk == b, tbl_ref[b, head], acc)
            o_ref[var, h] = acc


def attention_bias(rel_bias):
    n_pairs = rel_bias.shape[1] // HEADS_PER_TILE
    n_g = len(DILATIONS)
    return pl.pallas_call(
        _bias_body,
        grid=(n_pairs, n_g),
        in_specs=[pl.BlockSpec(memory_space=pltpu.SMEM),
                  pl.BlockSpec((None, 3, Q_BLOCK, K_BLOCK), lambda p, g: (g, 0, 0, 0))],
        out_specs=pl.BlockSpec((None, None, 3, HEADS_PER_TILE, Q_BLOCK, K_BLOCK),
                               lambda p, g: (p, g, 0, 0, 0, 0)),
        out_shape=jax.ShapeDtypeStruct((n_pairs, n_g, 3, HEADS_PER_TILE, Q_BLOCK, K_BLOCK), F32),
        compiler_params=_cparams(("parallel", "parallel")),
    )(rel_bias.astype(F32), _bias_buckets())


def _attn_body(q_ref, k_ref, v_ref, bias_ref, o_ref, acc_ref, m_ref, l_ref, s_scr):
    slab_len = q_ref.shape[1]
    seq = ATT_SLABS * slab_len
    lo = lax.broadcasted_iota(jnp.int32, (Q_BLOCK, LANES), 1) < HEAD_DIM
    ones = jnp.ones((K_BLOCK, LANES), BF16)

    def window(n, sub_len):
        qs = n * Q_BLOCK
        ks = jnp.clip(qs - ATT_SIDE, 0, sub_len - K_BLOCK)
        var = jnp.where(n > 0, 1, 0) + jnp.where(n == sub_len // Q_BLOCK - 1, 1, 0)
        return qs, ks, var

    def pieces(dil, it, u):
        if dil == 1:
            qs, ks, var = window(it * ATT_UNROLL + u, seq)
            q0 = pl.multiple_of(qs // ATT_SLABS, Q_BLOCK // ATT_SLABS)
            k0 = pl.multiple_of(ks // ATT_SLABS, ATT_SIDE // ATT_SLABS)
            return ([(r, pl.ds(q0, Q_BLOCK // ATT_SLABS)) for r in range(ATT_SLABS)],
                    [(r, pl.ds(k0, K_BLOCK // ATT_SLABS)) for r in range(ATT_SLABS)], var)
        if dil == ATT_SLABS:
            qs, ks, var = window(it, slab_len)
            return ([(u, pl.ds(pl.multiple_of(qs, Q_BLOCK), Q_BLOCK))],
                    [(u, pl.ds(pl.multiple_of(ks, ATT_SIDE), K_BLOCK))], var)
        step = dil // ATT_SLABS
        a, n = it % step, it // step
        qs, ks, var = window(n, slab_len // step)
        return ([(u, pl.ds(a + step * qs, Q_BLOCK, stride=step))],
                [(u, pl.ds(a + step * ks, K_BLOCK, stride=step))], var)

    def load(ref, idx):
        parts = [ref[r, rows, :] for r, rows in idx]
        return parts[0] if len(parts) == 1 else jnp.concatenate(parts, axis=0)

    def store(ref, idx, val):
        n = val.shape[0] // len(idx)
        for j, (r, rows) in enumerate(idx):
            ref[r, rows, :] = val[j * n:(j + 1) * n]

    def scores(g, q_idx, k_idx, var):
        q = load(q_ref, q_idx)
        k = load(k_ref, k_idx).astype(BF16)
        qq = jnp.concatenate([jnp.where(lo, q, 0.0), jnp.where(lo, 0.0, q)], axis=0).astype(BF16)
        return _nt(qq, k) + bias_ref[g, var]

    def weighted(k_idx, s):
        vo = jnp.concatenate([load(v_ref, k_idx).astype(BF16), ones], axis=1)
        m2 = jnp.max(s, axis=1, keepdims=True)
        ol = jnp.dot(jnp.exp(s - m2).astype(BF16), vo, preferred_element_type=F32)
        top, bot = ol[:Q_BLOCK], ol[Q_BLOCK:]
        m = jnp.where(lo, jnp.broadcast_to(m2[:Q_BLOCK], (Q_BLOCK, LANES)),
                      jnp.broadcast_to(m2[Q_BLOCK:], (Q_BLOCK, LANES)))
        return (jnp.where(lo, top[:, :LANES], bot[:, :LANES]),
                jnp.where(lo, top[:, LANES:], bot[:, LANES:]), m)

    def merge(g, q_idx, o, l, m):
        if g > 0:
            m_old = load(m_ref, q_idx)
            m_new = jnp.maximum(m_old, m)
            a_old, a_cur = jnp.exp(m_old - m_new), jnp.exp(m - m_new)
            o = load(acc_ref, q_idx) * a_old + o * a_cur
            l = load(l_ref, q_idx) * a_old + l * a_cur
            m = m_new
        store(acc_ref, q_idx, o)
        store(l_ref, q_idx, l)
        store(m_ref, q_idx, m)

    n_iter = seq // Q_BLOCK // ATT_UNROLL
    for g, dil in enumerate(DILATIONS):
        def score_into(slot, it, g=g, dil=dil):
            for u in range(ATT_UNROLL):
                q_idx, k_idx, var = pieces(dil, it, u)
                s_scr[slot, u] = scores(g, q_idx, k_idx, var)

        def step_pair(pair, carry, g=g, dil=dil, score_into=score_into):
            for slot in range(2):
                it = 2 * pair + slot
                score_into(1 - slot, jnp.minimum(it + 1, n_iter - 1))
                blocks = [pieces(dil, it, u) for u in range(ATT_UNROLL)]
                done = [weighted(k_idx, s_scr[slot, u]) for u, (_, k_idx, _) in enumerate(blocks)]
                for (q_idx, _, _), res in zip(blocks, done):
                    merge(g, q_idx, *res)
            return carry

        score_into(0, 0)
        lax.fori_loop(0, n_iter // 2, step_pair, 0)

    def normalise(i, c):
        start = pl.multiple_of(i * Q_BLOCK, Q_BLOCK)
        for r in range(ATT_SLABS):
            rows = pl.ds(start, Q_BLOCK)
            o_ref[pl.ds(r + ATT_SLABS * start, Q_BLOCK, stride=ATT_SLABS), :] = (
                acc_ref[r, rows, :] / l_ref[r, rows, :])
        return c

    lax.fori_loop(0, slab_len // Q_BLOCK, normalise, 0)


def attention(qkv, bias):
    bsz, n_slab, slab_len, _ = qkv.shape
    seq = n_slab * slab_len
    n_pairs = ATT_WIDTH // LANES
    assert n_slab == ATT_SLABS == ATT_UNROLL == DILATIONS[1] and DILATIONS[2] % ATT_SLABS == 0
    assert seq % (max(DILATIONS) * Q_BLOCK) == 0 and seq // max(DILATIONS) >= K_BLOCK
    bias = bias.reshape(bias.shape[:3] + (HEADS_PER_TILE * Q_BLOCK, K_BLOCK))
    slabs = (ATT_SLABS, slab_len, LANES)

    def col(c0):
        return pl.BlockSpec((None,) + slabs, lambda b, p: (b, 0, 0, c0 + p))

    return pl.pallas_call(
        _attn_body,
        grid=(bsz, n_pairs),
        in_specs=[col(0), col(n_pairs), col(2 * n_pairs),
                  pl.BlockSpec((None,) + bias.shape[1:], lambda b, p: (p, 0, 0, 0, 0))],
        out_specs=pl.BlockSpec((None, seq, LANES), lambda b, p: (b, 0, p)),
        out_shape=jax.ShapeDtypeStruct((bsz, seq, ATT_WIDTH), F32),
        scratch_shapes=[pltpu.VMEM(slabs, F32), pltpu.VMEM(slabs, F32), pltpu.VMEM(slabs, F32),
                        pltpu.VMEM((2, ATT_UNROLL, HEADS_PER_TILE * Q_BLOCK, K_BLOCK), F32)],
        compiler_params=_cparams(("parallel", "parallel")),
    )(qkv, qkv, qkv, bias)


POOL_HALO = max(POOL_WINDOWS) // 2


def _pooled(t, n_tiles, u_ref, prev_ref, next_ref, w_ref, scale_ref, pad_ref, *, seq):
    ts = u_ref.shape[0]
    u = u_ref[...]
    pad_ref[pl.ds(0, POOL_HALO), :] = jnp.where(t > 0, prev_ref[...], 0.0)
    pad_ref[pl.ds(POOL_HALO, ts), :] = u
    pad_ref[pl.ds(POOL_HALO + ts, POOL_HALO), :] = jnp.where(t < n_tiles - 1, next_ref[...], 0.0)

    def shifted(off):
        return pad_ref[pl.ds(POOL_HALO + off, ts), :]

    lane_win = lax.broadcasted_iota(jnp.int32, (1, POOL_WIDTH), 1) // HEAD_DIM
    pos = t * ts + lax.broadcasted_iota(jnp.int32, (ts, 1), 0)
    total = jnp.zeros_like(u)
    cnt = jnp.zeros_like(u)
    acc = None
    lo_off, hi_off = 0, 0
    for gi, win in enumerate(POOL_WINDOWS):
        for off in list(range(-(win // 2), lo_off)) + list(range(hi_off, win - win // 2)):
            acc = shifted(off) if acc is None else acc + shifted(off)
        lo_off, hi_off = -(win // 2), win - win // 2
        in_group = lane_win == gi
        n = (jnp.minimum(pos + hi_off, seq) - jnp.maximum(pos + lo_off, 0)).astype(F32)
        total = jnp.where(in_group, acc, total)
        cnt = jnp.where(in_group, n, cnt)
    mixed = jnp.dot((total / cnt - u).astype(BF16), w_ref[...], preferred_element_type=F32)
    return mixed * scale_ref[...]


def _pool_blockdiag(w_pool):
    n_g = w_pool.shape[0]
    eye = jnp.eye(n_g, dtype=w_pool.dtype)
    return jnp.einsum('gcd,gh->gchd', w_pool, eye).reshape(n_g * HEAD_DIM, n_g * HEAD_DIM)


def _hgrn_constants():
    c = HGRN_CHUNK
    t = np.arange(c)
    s_col = np.tile(t, HEADS_PER_TILE)
    cum, masks = [], []
    for direction in range(2):
        tri = (t[None, :] <= t[:, None]) if direction == 0 else (t[None, :] >= t[:, None])
        tri = tri.astype(np.float32)
        mats, qms, bms = [tri], [], []
        for level in range(HGRN_LEVELS):
            half = HGRN_SUB << level
            start = (t // (2 * half)) * (2 * half)
            boundary = start + (half - 1 if direction == 0 else half)
            mats.append(tri[boundary])
            q_right = (t // half) % 2 == 1
            qm = q_right if direction == 0 else ~q_right
            qms.append(np.broadcast_to(qm[:, None], (c, LANES)))
            bms.append(t[:, None] // (2 * half) == s_col[None, :] // (2 * half))
        same_sub = t[:, None] // HGRN_SUB == s_col[None, :] // HGRN_SUB
        causal = (s_col[None, :] <= t[:, None]) if direction == 0 else (s_col[None, :] >= t[:, None])
        cum.append(np.tile(np.concatenate(mats, axis=0), (1, 3)))
        masks.append(np.stack(qms + bms[:HGRN_LEVELS - 1] + [same_sub & causal]).astype(np.float32))
    lane_head = np.arange(LANES) // HEAD_DIM
    sel = np.concatenate([(lane_head[:, None] == lane_head[None, :]) & (s_col[None, :] % HGRN_SUB == j)
                          for j in range(HGRN_SUB)], axis=0).astype(np.float32)
    return (jnp.asarray(np.stack(cum), BF16), jnp.asarray(np.stack(masks), F32), jnp.asarray(sel, BF16))


def _split3(x):
    x1 = x.astype(BF16)
    r1 = x - x1.astype(F32)
    x2 = r1.astype(BF16)
    x3 = (r1 - x2.astype(F32)).astype(BF16)
    return x1, x2, x3


def _by_head(x):
    lo = lax.broadcasted_iota(jnp.int32, x.shape, 1) < HEAD_DIM
    return jnp.concatenate([jnp.where(lo, x, 0.0), jnp.where(lo, 0.0, x)], axis=0).astype(BF16)


def _hgrn_gates(st, slot, cum_ref, ball_scr, k_scr):
    fl = st["f_ref"][st["rows"], st["cols"]]
    lb = st["lb_ref"][:, st["cols"]]
    e = jnp.exp(-jnp.abs(fl))
    one_plus_e = 1.0 + e
    log_sig = jnp.minimum(fl, 0.0) - jnp.log(one_plus_e)
    rcp = 1.0 / one_plus_e
    k_scr[slot, st["stream"]] = (1.0 - lb) * jnp.where(fl >= 0, e * rcp, rcp)
    log_lb = jnp.log(lb)
    cc = jnp.log(1.0 - lb) + log_sig
    g = jnp.maximum(log_lb, cc) + jnp.log(1.0 + jnp.exp(-jnp.abs(log_lb - cc)))
    cum = cum_ref[st["direction"]]
    ball_scr[slot, st["stream"]] = jnp.dot(cum, jnp.concatenate(_split3(g * LOG2_E), axis=0),
                                           preferred_element_type=F32)


def _hgrn_products(st, slot, mask_ref, sel_ref, state_ref, ball_scr, k_scr, out_scr):
    c = HGRN_CHUNK
    direction, stream = st["direction"], st["stream"]
    q, v = st["q_ref"][st["rows"], st["cols"]], st["v_ref"][st["rows"], st["cols"]]
    kk = k_scr[slot, stream]
    ball = ball_scr[slot, stream]
    b = ball[0:c]
    b_edge = b[c - 1:c] if direction == 0 else b[0:1]

    state = state_ref[stream]
    out_scr[slot, stream, 0] = _nt((q * jnp.exp2(b)).astype(BF16), state.astype(BF16))
    k_out = kk * jnp.exp2(b_edge - b)
    upd = _tn(v.astype(BF16), k_out.astype(BF16))
    r128 = lax.broadcasted_iota(jnp.int32, (LANES, LANES), 0) // HEAD_DIM
    c128 = lax.broadcasted_iota(jnp.int32, (LANES, LANES), 1) // HEAD_DIM
    state_ref[stream] = jnp.where(r128 == c128, state * jnp.exp2(b_edge) + upd, 0.0)

    att = None
    for level in range(HGRN_LEVELS):
        beta = ball[(level + 1) * c:(level + 2) * c]
        qm = mask_ref[direction, level]
        decay = jnp.exp2(-jnp.abs(b - beta))
        ql = q * decay * qm
        kl = kk * decay * (1.0 - qm)
        a = _nt(ql.astype(BF16), _by_head(kl))
        if level < HGRN_LEVELS - 1:
            a = a * mask_ref[direction, HGRN_LEVELS + level]
        att = a if att is None else att + a

    pair_cols = []
    for j in range(HGRN_SUB):
        pieces = []
        for u in range(c // HGRN_SUB):
            r = u * HGRN_SUB
            b_row = ball_scr[slot, stream, pl.ds(r + j, 1), :]
            k_row = k_scr[slot, stream, pl.ds(r + j, 1), :]
            pieces.append((q[r:r + HGRN_SUB] * k_row)
                          * jnp.exp2(jnp.minimum(b[r:r + HGRN_SUB] - b_row, 0.0)))
        pair_cols.append(jnp.concatenate(pieces, axis=0).astype(BF16))
    diag = jnp.dot(jnp.concatenate(pair_cols, axis=1), sel_ref[...], preferred_element_type=F32)
    out_scr[slot, stream, 1] = att + diag * mask_ref[direction, 2 * HGRN_LEVELS - 1]


def _hgrn_finish(st, slot, out_scr):
    v = st["v_ref"][st["rows"], st["cols"]]
    st["o_ref"][st["rows"], st["cols"]] = out_scr[slot, st["stream"], 0] + jnp.dot(
        out_scr[slot, st["stream"], 1].astype(BF16), _by_head(v), preferred_element_type=F32)


def _hgrn_body(qf_ref, vf_ref, ff_ref, qb_ref, vb_ref, fb_ref, lbf_ref, lbb_ref,
               cum_ref, mask_ref, sel_ref, of_ref, ob_ref, state_ref, ball_scr, k_scr, out_scr):
    @pl.when(pl.program_id(1) == 0)
    def _():
        state_ref[...] = jnp.zeros_like(state_ref)

    n_chunks = qf_ref.shape[0] // HGRN_CHUNK
    n_tiles = HGRN_WIDTH // LANES
    per_direction = ((qf_ref, vf_ref, ff_ref, lbf_ref, of_ref), (qb_ref, vb_ref, fb_ref, lbb_ref, ob_ref))

    def streams_of(i):
        streams = []
        for direction, (q_ref, v_ref, f_ref, lb_ref, o_ref) in enumerate(per_direction):
            chunk = i if direction == 0 else n_chunks - 1 - i
            rows = pl.ds(pl.multiple_of(chunk * HGRN_CHUNK, HGRN_CHUNK), HGRN_CHUNK)
            for tile in range(n_tiles):
                streams.append(dict(direction=direction, stream=direction * n_tiles + tile, rows=rows,
                                    cols=slice(tile * LANES, (tile + 1) * LANES), q_ref=q_ref,
                                    v_ref=v_ref, f_ref=f_ref, lb_ref=lb_ref, o_ref=o_ref))
        return streams

    for st in streams_of(0):
        _hgrn_gates(st, 0, cum_ref, ball_scr, k_scr)
    out_scr[1] = jnp.zeros(out_scr.shape[1:], out_scr.dtype)

    def step_group(group, carry):
        for k in range(HGRN_UNROLL):
            slot = k % 2
            i = HGRN_UNROLL * group + k
            for st in streams_of(jnp.maximum(i - 1, 0)):
                _hgrn_finish(st, 1 - slot, out_scr)
            for st in streams_of(jnp.minimum(i + 1, n_chunks - 1)):
                _hgrn_gates(st, 1 - slot, cum_ref, ball_scr, k_scr)
            for st in streams_of(i):
                _hgrn_products(st, slot, mask_ref, sel_ref, state_ref, ball_scr, k_scr, out_scr)
        return carry

    lax.fori_loop(0, n_chunks // HGRN_UNROLL, step_group, 0)
    for st in streams_of(n_chunks - 1):
        _hgrn_finish(st, (n_chunks - 1) % 2, out_scr)


def hgrn_scan(rest, lb_fwd, lb_bwd, *, ts=1024):
    bsz, seq, _ = rest.shape
    ts = min(ts, seq)
    nt = seq // ts
    n_streams = 2 * HGRN_WIDTH // LANES
    cum, masks, sel = _hgrn_constants()

    def fwd(c0):
        return pl.BlockSpec((None, ts, HGRN_WIDTH), lambda b, j: (b, j, c0))

    def bwd(c0):
        return pl.BlockSpec((None, ts, HGRN_WIDTH), lambda b, j: (b, nt - 1 - j, c0))

    return pl.pallas_call(
        _hgrn_body,
        grid=(bsz, nt),
        in_specs=[fwd(1), fwd(2), fwd(3), bwd(1), bwd(2), bwd(4),
                  _resident((1, HGRN_WIDTH)), _resident((1, HGRN_WIDTH)),
                  _resident(cum.shape), _resident(masks.shape), _resident(sel.shape)],
        out_specs=[fwd(0), bwd(0)],
        out_shape=[jax.ShapeDtypeStruct((bsz, seq, HGRN_WIDTH), F32)] * 2,
        scratch_shapes=[pltpu.VMEM((n_streams, LANES, LANES), F32),
                        pltpu.VMEM((2, n_streams) + cum.shape[1:2] + (LANES,), F32),
                        pltpu.VMEM((2, n_streams, HGRN_CHUNK, LANES), F32),
                        pltpu.VMEM((2, n_streams, 2, HGRN_CHUNK, LANES), F32)],
        compiler_params=_cparams(("parallel", "arbitrary")),
    )(rest, rest, rest, rest, rest, rest, lb_fwd.reshape(1, HGRN_WIDTH), lb_bwd.reshape(1, HGRN_WIDTH),
      cum, masks, sel)


def _post_body(x_ref, att_ref, u_ref, prev_ref, next_ref, of_ref, ob_ref, gate_ref,
               pw_ref, ps_ref, gain_ref, bd_ref, w_ref, g2_ref, wg_ref, wu_ref, wd_ref,
               o_ref, pad_ref, a_ref, mixed_scr, *, seq, tiles, n_tiles):
    step = pl.program_id(0)

    @pl.when(step == 0)
    def _():
        mixed_scr[...] = jnp.zeros_like(mixed_scr)

    mixed = []

    def epilogues():
        pooled = _pooled(jnp.minimum(step, n_tiles - 1) % tiles, tiles, u_ref, prev_ref, next_ref,
                         pw_ref, ps_ref, pad_ref, seq=seq)
        o = of_ref[...] + ob_ref[...]
        y = o * lax.rsqrt(_head_mean_sq(o, bd_ref[...]) + EPS) * gain_ref[...]
        gate = gate_ref[...]
        rec = (y * (gate * jax.nn.sigmoid(gate))).astype(BF16)
        mixed.append(jnp.concatenate([att_ref[...].astype(BF16), pooled.astype(BF16), rec], axis=1))

    x = x_ref[...] + jnp.dot(mixed_scr[...], w_ref[...], preferred_element_type=F32)
    o_ref[...] = _half_ffn(x, g2_ref, wg_ref, wu_ref, wd_ref, a_ref,
                           between=[None] * (wg_ref.shape[1] // MXU_COLS - 1) + [epilogues])
    mixed_scr[...] = mixed[0]


def post_mixer(x3, y_att, rest, o_fwd, o_bwd, pool_w, pool_scale, out_gain, w_out,
               ffn_gain, wg, wu, wd, *, tm=512):
    bsz, seq, d = x3.shape
    dff = wg.shape[1]
    tm = min(tm, seq)
    tiles = seq // tm
    n_tiles = bsz * tiles
    gate_tile = (rest.shape[2] - HGRN_WIDTH) // HGRN_WIDTH
    per_tile = tm // POOL_HALO
    last_halo = seq // POOL_HALO - 1

    def ffn_block(width):
        def index(i):
            j = jnp.maximum(i - 1, 0)
            return j // tiles, j % tiles, 0
        return pl.BlockSpec((None, tm, width), index)

    def mix_block(width, col=0, rows=tm, row_index=lambda t: t):
        def index(i):
            j = jnp.minimum(i, n_tiles - 1)
            return j // tiles, row_index(j % tiles), col
        return pl.BlockSpec((None, rows, width), index)

    return pl.pallas_call(
        functools.partial(_post_body, seq=seq, tiles=tiles, n_tiles=n_tiles),
        grid=(n_tiles + 1,),
        in_specs=[ffn_block(d), mix_block(ATT_WIDTH), mix_block(POOL_WIDTH),
                  mix_block(POOL_WIDTH, rows=POOL_HALO,
                            row_index=lambda t: jnp.maximum(t * per_tile - 1, 0)),
                  mix_block(POOL_WIDTH, rows=POOL_HALO,
                            row_index=lambda t: jnp.minimum((t + 1) * per_tile, last_halo)),
                  mix_block(HGRN_WIDTH), mix_block(HGRN_WIDTH), mix_block(HGRN_WIDTH, gate_tile),
                  _resident((POOL_WIDTH, POOL_WIDTH)), _resident((1, POOL_WIDTH)),
                  _resident((1, HGRN_WIDTH)), _resident((HGRN_WIDTH, HGRN_WIDTH)), _resident(w_out.shape),
                  _resident((1, d)), _resident((d, dff)), _resident((d, dff)), _resident((dff, d))],
        out_specs=ffn_block(d),
        out_shape=jax.ShapeDtypeStruct((bsz, seq, d), F32),
        scratch_shapes=[pltpu.VMEM((tm + 2 * POOL_HALO, POOL_WIDTH), F32), pltpu.VMEM((tm, dff), BF16),
                        pltpu.VMEM((tm, w_out.shape[0]), BF16)],
        compiler_params=_cparams(("arbitrary",)),
    )(x3, y_att, rest, rest, rest, o_fwd, o_bwd, rest,
      _pool_blockdiag(pool_w).astype(BF16), pool_scale.reshape(1, POOL_WIDTH),
      jnp.tile(out_gain, HGRN_WIDTH // HEAD_DIM).reshape(1, HGRN_WIDTH), _head_blockdiag(HGRN_WIDTH),
      w_out, ffn_gain.reshape(1, d), wg, wu, wd)


def kernel(x, ffn1_norm, ffn1_w_gate, ffn1_w_up, ffn1_w_down, mix_norm, w_in, q_norm, k_norm, rel_bias,
           pool_w, pool_scale, hgrn_lb_logits, hgrn_norm, w_out, ffn2_norm, ffn2_w_gate, ffn2_w_up,
           ffn2_w_down):
    bsz, seq, d = x.shape
    depth = w_in.shape[0]
    n = bsz * seq
    h = x.astype(F32).reshape(n, d)
    bias = attention_bias(rel_bias)
    lb_cum = jnp.cumsum(jax.nn.softmax(hgrn_lb_logits.astype(F32), axis=1), axis=1)
    lb_all = lb_cum - lb_cum[:, :1]
    for l in range(depth):
        h, qkv, rest = pre_mixer(h, seq, ffn1_norm[l], ffn1_w_gate[l].astype(BF16),
                                 ffn1_w_up[l].astype(BF16), ffn1_w_down[l].astype(BF16),
                                 mix_norm[l], w_in[l].astype(BF16), q_norm[l], k_norm[l])
        rest = rest.reshape(bsz, seq, -1)
        y_att = attention(qkv, bias)
        o_fwd, o_bwd = hgrn_scan(rest, lb_all[0, l], lb_all[1, l])
        h = post_mixer(h.reshape(bsz, seq, d), y_att, rest, o_fwd, o_bwd, pool_w[l], pool_scale[l],
                       hgrn_norm[l], w_out[l].astype(BF16), ffn2_norm[l], ffn2_w_gate[l].astype(BF16),
                       ffn2_w_up[l].astype(BF16), ffn2_w_down[l].astype(BF16)).reshape(n, d)
    return h.reshape(bsz, seq, d).astype(x.dtype)
```

```python
import functools
import math

import numpy as np
import jax
import jax.numpy as jnp
from jax import lax
from jax.experimental import pallas as pl
from jax.experimental.pallas import tpu as pltpu

F32 = jnp.float32
BF16 = jnp.bfloat16

LANES = 128
SUBLANES = 8
MXU_COLS = 256
VMEM_BYTES_V7X = 64 * 1024 * 1024
VMEM_LIMIT = VMEM_BYTES_V7X - 8 * 1024 * 1024

HEAD_DIM = 64
HEADS_PER_TILE = LANES // HEAD_DIM
ATT_WIDTH = 512
POOL_WIDTH = 256
HGRN_WIDTH = 256
POOL_WINDOWS = (2, 4, 8, 16)
DILATIONS = (1, 4, 16)
ATT_SIDE = 64
NUM_BUCKETS = 32
MAX_DISTANCE = 1024
EPS = 1e-6
NEG = -1e30
LOG2_E = math.log2(math.e)

Q_BLOCK = 128
K_BLOCK = Q_BLOCK + 2 * ATT_SIDE
ATT_UNROLL = 4
ATT_SLABS = 4
HGRN_CHUNK = 64
HGRN_SUB = SUBLANES
HGRN_LEVELS = 3
HGRN_UNROLL = 4


def _cparams(sem):
    return pltpu.CompilerParams(dimension_semantics=sem, vmem_limit_bytes=VMEM_LIMIT)


def _resident(shape):
    nd = len(shape)
    return pl.BlockSpec(shape, lambda *_: (0,) * nd, pipeline_mode=pl.Buffered(1))


def _nt(a, b):
    return lax.dot_general(a, b, (((1,), (1,)), ((), ())), preferred_element_type=F32)


def _tn(a, b):
    return lax.dot_general(a, b, (((0,), (0,)), ((), ())), preferred_element_type=F32)


def _head_blockdiag(width=LANES):
    r = np.arange(width) // HEAD_DIM
    return jnp.asarray(r[:, None] == r[None, :], dtype=BF16)


def _head_mean_sq(v, bd):
    return jnp.dot((v * v).astype(BF16), bd, preferred_element_type=F32) * (1.0 / HEAD_DIM)


def _rmsnorm_bf16(x, gain):
    return (x * lax.rsqrt(jnp.mean(x * x, axis=-1, keepdims=True) + EPS) * gain).astype(BF16)


def _half_ffn(x, g_ref, wg_ref, wu_ref, wd_ref, a_ref, between=()):
    h = _rmsnorm_bf16(x, g_ref[...])
    between = list(between)
    for c in range(wg_ref.shape[1] // MXU_COLS):
        sl = slice(c * MXU_COLS, (c + 1) * MXU_COLS)
        gate = jnp.dot(h, wg_ref[:, sl], preferred_element_type=F32)
        up = jnp.dot(h, wu_ref[:, sl], preferred_element_type=F32)
        a_ref[:, sl] = (gate * jax.nn.sigmoid(gate) * up).astype(BF16)
        thunk = between.pop(0) if between else None
        if thunk is not None:
            thunk()
    for thunk in between:
        thunk()
    return x + 0.5 * jnp.dot(a_ref[...], wd_ref[...], preferred_element_type=F32)


def _pre_body(x_ref, g1_ref, wg_ref, wu_ref, wd_ref, g_ref, w_ref, qg_ref, kg_ref, bd_ref,
              h_ref, qkv_ref, rest_ref, a_ref, perm_scr, hn_scr):
    @pl.when(pl.program_id(0) == 0)
    def _():
        hn_scr[...] = jnp.zeros_like(hn_scr)

    bd = bd_ref[...]
    wc = MXU_COLS
    slab_rows = qkv_ref.shape[1]
    n_att = 3 * ATT_WIDTH

    def to_slabs(c, z):
        for t in range(wc // LANES):
            perm_scr[c, t] = z[:, t * LANES:(t + 1) * LANES]
            for r in range(ATT_SLABS):
                lanes = slice(c * wc + t * LANES, c * wc + (t + 1) * LANES)
                qkv_ref[r, :, lanes] = perm_scr[c, t, pl.ds(r, slab_rows, stride=ATT_SLABS), :]

    projected = {}

    hn = hn_scr[...]

    def project(c):
        projected[c] = jnp.dot(hn, w_ref[:, c * wc:(c + 1) * wc], preferred_element_type=F32)

    def finish(c):
        sl = slice(c * wc, (c + 1) * wc)
        z = projected.pop(c)
        if c >= n_att // wc:
            rest_ref[:, c * wc - n_att:(c + 1) * wc - n_att] = z
            return
        if c < 2 * ATT_WIDTH // wc:
            if c < ATT_WIDTH // wc:
                gain = qg_ref[:, sl] * (HEAD_DIM ** -0.5)
            else:
                gain = kg_ref[:, c * wc - ATT_WIDTH:(c + 1) * wc - ATT_WIDTH]
            z = z * lax.rsqrt(_head_mean_sq(z, bd) + EPS) * gain
        to_slabs(c, z)

    def skewed(c):
        def thunk():
            if c > 0:
                finish(c - 1)
            if c < n_chunks:
                project(c)
        return thunk

    n_chunks = w_ref.shape[1] // wc
    thunks = [skewed(c) for c in range(n_chunks + 1)]
    lead, trail = 2, 3
    for thunk in thunks[:lead]:
        thunk()
    x = _half_ffn(x_ref[...], g1_ref, wg_ref, wu_ref, wd_ref, a_ref, between=thunks[lead:-trail])
    for thunk in thunks[-trail:]:
        thunk()
    h_ref[...] = x
    hn_scr[...] = _rmsnorm_bf16(x, g_ref[...])


def pre_mixer(x2, seq, ffn_gain, wg, wu, wd, gain, w_in, q_gain, k_gain, *, tm=512):
    n, d = x2.shape
    dff = wg.shape[1]
    cols = w_in.shape[1]
    n_att = 3 * ATT_WIDTH
    tm = min(tm, seq)
    tiles = seq // tm
    last = n // tm - 1

    def ffn_tile(i):
        return jnp.minimum(i, last)

    def proj_tile(i):
        return jnp.maximum(i - 1, 0)

    return pl.pallas_call(
        _pre_body,
        grid=(n // tm + 1,),
        in_specs=[pl.BlockSpec((tm, d), lambda i: (ffn_tile(i), 0)),
                  _resident((1, d)), _resident((d, dff)), _resident((d, dff)), _resident((dff, d)),
                  _resident((1, d)), _resident((d, cols)),
                  _resident((1, ATT_WIDTH)), _resident((1, ATT_WIDTH)), _resident((MXU_COLS, MXU_COLS))],
        out_specs=[pl.BlockSpec((tm, d), lambda i: (ffn_tile(i), 0)),
                   pl.BlockSpec((None, ATT_SLABS, tm // ATT_SLABS, n_att),
                                lambda i: (proj_tile(i) // tiles, 0, proj_tile(i) % tiles, 0)),
                   pl.BlockSpec((tm, cols - n_att), lambda i: (proj_tile(i), 0))],
        out_shape=[jax.ShapeDtypeStruct((n, d), F32),
                   jax.ShapeDtypeStruct((n // seq, ATT_SLABS, seq // ATT_SLABS, n_att), F32),
                   jax.ShapeDtypeStruct((n, cols - n_att), F32)],
        scratch_shapes=[pltpu.VMEM((tm, dff), BF16),
                        pltpu.VMEM((n_att // MXU_COLS, MXU_COLS // LANES, tm, LANES), F32),
                        pltpu.VMEM((tm, d), BF16)],
        compiler_params=_cparams(("arbitrary",)),
    )(x2, ffn_gain.reshape(1, d), wg, wu, wd, gain.reshape(1, d), w_in,
      q_gain.reshape(1, ATT_WIDTH), k_gain.reshape(1, ATT_WIDTH), _head_blockdiag(MXU_COLS))


def _t5_bucket(rel):
    half = NUM_BUCKETS // 2
    max_exact = half // 2
    base = jnp.where(rel > 0, half, 0)
    n = jnp.abs(rel)
    nf = jnp.maximum(n, 1).astype(F32)
    large = max_exact + (jnp.log(nf / max_exact) / math.log(MAX_DISTANCE / max_exact)
                         * (half - max_exact)).astype(jnp.int32)
    large = jnp.minimum(large, half - 1)
    return base + jnp.where(n < max_exact, n, large)


def _bias_buckets():
    out = []
    for dil in DILATIONS:
        qi, ki = np.arange(Q_BLOCK), np.arange(K_BLOCK)
        if dil == 1:
            qi = qi.reshape(-1, ATT_SLABS).T.reshape(-1)
            ki = ki.reshape(-1, ATT_SLABS).T.reshape(-1)
        qi, ki = jnp.asarray(qi)[:, None], jnp.asarray(ki)[None, :]
        per_var = []
        for var in range(3):
            rel = ki - var * ATT_SIDE - qi
            per_var.append(jnp.where(jnp.abs(rel) <= ATT_SIDE, _t5_bucket(rel * dil), -1))
        out.append(jnp.stack(per_var))
    return jnp.stack(out).astype(jnp.int32)


def _bias_body(tbl_ref, bk_ref, o_ref):
    hp = pl.program_id(0)
    for var in range(3):
        bk = bk_ref[var]
        for h in range(HEADS_PER_TILE):
            head = hp * HEADS_PER_TILE + h
            acc = jnp.full(bk.shape, NEG, F32)
            for b in range(NUM_BUCKETS):
                acc = jnp.where(bk == b, tbl_ref[b, head], acc)
            o_ref[var, h] = acc


def attention_bias(rel_bias):
    n_pairs = rel_bias.shape[1] // HEADS_PER_TILE
    n_g = len(DILATIONS)
    return pl.pallas_call(
        _bias_body,
        grid=(n_pairs, n_g),
        in_specs=[pl.BlockSpec(memory_space=pltpu.SMEM),
                  pl.BlockSpec((None, 3, Q_BLOCK, K_BLOCK), lambda p, g: (g, 0, 0, 0))],
        out_specs=pl.BlockSpec((None, None, 3, HEADS_PER_TILE, Q_BLOCK, K_BLOCK),
                               lambda p, g: (p, g, 0, 0, 0, 0)),
        out_shape=jax.ShapeDtypeStruct((n_pairs, n_g, 3, HEADS_PER_TILE, Q_BLOCK, K_BLOCK), F32),
        compiler_params=_cparams(("parallel", "parallel")),
    )(rel_bias.astype(F32), _bias_buckets())


def _attn_body(q_ref, k_ref, v_ref, bias_ref, o_ref, acc_ref, m_ref, l_ref, s_scr):
    slab_len = q_ref.shape[1]
    seq = ATT_SLABS * slab_len
    lo = lax.broadcasted_iota(jnp.int32, (Q_BLOCK, LANES), 1) < HEAD_DIM
    ones = jnp.ones((K_BLOCK, LANES), BF16)

    def window(n, sub_len):
        qs = n * Q_BLOCK
        ks = jnp.clip(qs - ATT_SIDE, 0, sub_len - K_BLOCK)
        var = jnp.where(n > 0, 1, 0) + jnp.where(n == sub_len // Q_BLOCK - 1, 1, 0)
        return qs, ks, var

    def pieces(dil, it, u):
        if dil == 1:
            qs, ks, var = window(it * ATT_UNROLL + u, seq)
            q0 = pl.multiple_of(qs // ATT_SLABS, Q_BLOCK // ATT_SLABS)
            k0 = pl.multiple_of(ks // ATT_SLABS, ATT_SIDE // ATT_SLABS)
            return ([(r, pl.ds(q0, Q_BLOCK // ATT_SLABS)) for r in range(ATT_SLABS)],
                    [(r, pl.ds(k0, K_BLOCK // ATT_SLABS)) for r in range(ATT_SLABS)], var)
        if dil == ATT_SLABS:
            qs, ks, var = window(it, slab_len)
            return ([(u, pl.ds(pl.multiple_of(qs, Q_BLOCK), Q_BLOCK))],
                    [(u, pl.ds(pl.multiple_of(ks, ATT_SIDE), K_BLOCK))], var)
        step = dil // ATT_SLABS
        a, n = it % step, it // step
        qs, ks, var = window(n, slab_len // step)
        return ([(u, pl.ds(a + step * qs, Q_BLOCK, stride=step))],
                [(u, pl.ds(a + step * ks, K_BLOCK, stride=step))], var)

    def load(ref, idx):
        parts = [ref[r, rows, :] for r, rows in idx]
        return parts[0] if len(parts) == 1 else jnp.concatenate(parts, axis=0)

    def store(ref, idx, val):
        n = val.shape[0] // len(idx)
        for j, (r, rows) in enumerate(idx):
            ref[r, rows, :] = val[j * n:(j + 1) * n]

    def scores(g, q_idx, k_idx, var):
        q = load(q_ref, q_idx)
        k = load(k_ref, k_idx).astype(BF16)
        qq = jnp.concatenate([jnp.where(lo, q, 0.0), jnp.where(lo, 0.0, q)], axis=0).astype(BF16)
        return _nt(qq, k) + bias_ref[g, var]

    def weighted(k_idx, s):
        vo = jnp.concatenate([load(v_ref, k_idx).astype(BF16), ones], axis=1)
        m2 = jnp.max(s, axis=1, keepdims=True)
        ol = jnp.dot(jnp.exp(s - m2).astype(BF16), vo, preferred_element_type=F32)
        top, bot = ol[:Q_BLOCK], ol[Q_BLOCK:]
        m = jnp.where(lo, jnp.broadcast_to(m2[:Q_BLOCK], (Q_BLOCK, LANES)),
                      jnp.broadcast_to(m2[Q_BLOCK:], (Q_BLOCK, LANES)))
        return (jnp.where(lo, top[:, :LANES], bot[:, :LANES]),
                jnp.where(lo, top[:, LANES:], bot[:, LANES:]), m)

    def merge(g, q_idx, o, l, m):
        if g > 0:
            m_old = load(m_ref, q_idx)
            m_new = jnp.maximum(m_old, m)
            a_old, a_cur = jnp.exp(m_old - m_new), jnp.exp(m - m_new)
            o = load(acc_ref, q_idx) * a_old + o * a_cur
            l = load(l_ref, q_idx) * a_old + l * a_cur
            m = m_new
        store(acc_ref, q_idx, o)
        store(l_ref, q_idx, l)
        store(m_ref, q_idx, m)

    n_iter = seq // Q_BLOCK // ATT_UNROLL
    for g, dil in enumerate(DILATIONS):
        def score_into(slot, it, g=g, dil=dil):
            for u in range(ATT_UNROLL):
                q_idx, k_idx, var = pieces(dil, it, u)
                s_scr[slot, u] = scores(g, q_idx, k_idx, var)

        def step_pair(pair, carry, g=g, dil=dil, score_into=score_into):
            for slot in range(2):
                it = 2 * pair + slot
                score_into(1 - slot, jnp.minimum(it + 1, n_iter - 1))
                blocks = [pieces(dil, it, u) for u in range(ATT_UNROLL)]
                done = [weighted(k_idx, s_scr[slot, u]) for u, (_, k_idx, _) in enumerate(blocks)]
                for (q_idx, _, _), res in zip(blocks, done):
                    merge(g, q_idx, *res)
            return carry

        score_into(0, 0)
        lax.fori_loop(0, n_iter // 2, step_pair, 0)

    def normalise(i, c):
        start = pl.multiple_of(i * Q_BLOCK, Q_BLOCK)
        for r in range(ATT_SLABS):
            rows = pl.ds(start, Q_BLOCK)
            o_ref[pl.ds(r + ATT_SLABS * start, Q_BLOCK, stride=ATT_SLABS), :] = (
                acc_ref[r, rows, :] / l_ref[r, rows, :])
        return c

    lax.fori_loop(0, slab_len // Q_BLOCK, normalise, 0)


def attention(qkv, bias):
    bsz, n_slab, slab_len, _ = qkv.shape
    seq = n_slab * slab_len
    n_pairs = ATT_WIDTH // LANES
    assert n_slab == ATT_SLABS == ATT_UNROLL == DILATIONS[1] and DILATIONS[2] % ATT_SLABS == 0
    assert seq % (max(DILATIONS) * Q_BLOCK) == 0 and seq // max(DILATIONS) >= K_BLOCK
    bias = bias.reshape(bias.shape[:3] + (HEADS_PER_TILE * Q_BLOCK, K_BLOCK))
    slabs = (ATT_SLABS, slab_len, LANES)

    def col(c0):
        return pl.BlockSpec((None,) + slabs, lambda b, p: (b, 0, 0, c0 + p))

    return pl.pallas_call(
        _attn_body,
        grid=(bsz, n_pairs),
        in_specs=[col(0), col(n_pairs), col(2 * n_pairs),
                  pl.BlockSpec((None,) + bias.shape[1:], lambda b, p: (p, 0, 0, 0, 0))],
        out_specs=pl.BlockSpec((None, seq, LANES), lambda b, p: (b, 0, p)),
        out_shape=jax.ShapeDtypeStruct((bsz, seq, ATT_WIDTH), F32),
        scratch_shapes=[pltpu.VMEM(slabs, F32), pltpu.VMEM(slabs, F32), pltpu.VMEM(slabs, F32),
                        pltpu.VMEM((2, ATT_UNROLL, HEADS_PER_TILE * Q_BLOCK, K_BLOCK), F32)],
        compiler_params=_cparams(("parallel", "parallel")),
    )(qkv, qkv, qkv, bias)


POOL_HALO = max(POOL_WINDOWS) // 2


def _pool_pad(t, n_tiles, u_ref, prev_ref, next_ref, pad_ref):
    ts = u_ref.shape[0]
    pad_ref[pl.ds(0, POOL_HALO), :] = jnp.where(t > 0, prev_ref[...], 0.0)
    pad_ref[pl.ds(POOL_HALO, ts), :] = u_ref[...]
    pad_ref[pl.ds(POOL_HALO + ts, POOL_HALO), :] = jnp.where(t < n_tiles - 1, next_ref[...], 0.0)


def _pool_centred(t, r0, rows, pad_ref, *, seq):
    ts = pad_ref.shape[0] - 2 * POOL_HALO

    def shifted(off):
        return pad_ref[pl.ds(POOL_HALO + r0 + off, rows), :]

    u = shifted(0)
    lane_win = lax.broadcasted_iota(jnp.int32, (1, POOL_WIDTH), 1) // HEAD_DIM
    pos = t * ts + r0 + lax.broadcasted_iota(jnp.int32, (rows, 1), 0)
    total = jnp.zeros_like(u)
    cnt = jnp.zeros_like(u)
    acc = None
    lo_off, hi_off = 0, 0
    for gi, win in enumerate(POOL_WINDOWS):
        for off in list(range(-(win // 2), lo_off)) + list(range(hi_off, win - win // 2)):
            acc = shifted(off) if acc is None else acc + shifted(off)
        lo_off, hi_off = -(win // 2), win - win // 2
        in_group = lane_win == gi
        n = (jnp.minimum(pos + hi_off, seq) - jnp.maximum(pos + lo_off, 0)).astype(F32)
        total = jnp.where(in_group, acc, total)
        cnt = jnp.where(in_group, n, cnt)
    return total / cnt - u


def _pool_blockdiag(w_pool):
    n_g = w_pool.shape[0]
    eye = jnp.eye(n_g, dtype=w_pool.dtype)
    return jnp.einsum('gcd,gh->gchd', w_pool, eye).reshape(n_g * HEAD_DIM, n_g * HEAD_DIM)


def _hgrn_constants():
    c = HGRN_CHUNK
    t = np.arange(c)
    s_col = np.tile(t, HEADS_PER_TILE)
    cum, masks = [], []
    for direction in range(2):
        tri = (t[None, :] <= t[:, None]) if direction == 0 else (t[None, :] >= t[:, None])
        tri = tri.astype(np.float32)
        mats, qms, bms = [tri], [], []
        for level in range(HGRN_LEVELS):
            half = HGRN_SUB << level
            start = (t // (2 * half)) * (2 * half)
            boundary = start + (half - 1 if direction == 0 else half)
            mats.append(tri[boundary])
            q_right = (t // half) % 2 == 1
            qm = q_right if direction == 0 else ~q_right
            qms.append(np.broadcast_to(qm[:, None], (c, LANES)))
            bms.append(t[:, None] // (2 * half) == s_col[None, :] // (2 * half))
        same_sub = t[:, None] // HGRN_SUB == s_col[None, :] // HGRN_SUB
        causal = (s_col[None, :] <= t[:, None]) if direction == 0 else (s_col[None, :] >= t[:, None])
        cum.append(np.tile(np.concatenate(mats, axis=0), (1, 3)))
        masks.append(np.stack(qms + bms[:HGRN_LEVELS - 1] + [same_sub & causal]).astype(np.float32))
    lane_head = np.arange(LANES) // HEAD_DIM
    sel = np.concatenate([(lane_head[:, None] == lane_head[None, :]) & (s_col[None, :] % HGRN_SUB == j)
                          for j in range(HGRN_SUB)], axis=0).astype(np.float32)
    return (jnp.asarray(np.stack(cum), BF16), jnp.asarray(np.stack(masks), F32), jnp.asarray(sel, BF16))


def _split3(x):
    x1 = x.astype(BF16)
    r1 = x - x1.astype(F32)
    x2 = r1.astype(BF16)
    x3 = (r1 - x2.astype(F32)).astype(BF16)
    return x1, x2, x3


def _by_head(x):
    lo = lax.broadcasted_iota(jnp.int32, x.shape, 1) < HEAD_DIM
    return jnp.concatenate([jnp.where(lo, x, 0.0), jnp.where(lo, 0.0, x)], axis=0).astype(BF16)


def _hgrn_gates(st, slot, cum_ref, ball_scr, k_scr):
    fl = st["f_ref"][st["rows"], st["cols"]]
    lb = st["lb_ref"][:, st["cols"]]
    e = jnp.exp(-jnp.abs(fl))
    one_plus_e = 1.0 + e
    log_sig = jnp.minimum(fl, 0.0) - jnp.log(one_plus_e)
    rcp = 1.0 / one_plus_e
    k_scr[slot, st["stream"]] = (1.0 - lb) * jnp.where(fl >= 0, e * rcp, rcp)
    log_lb = jnp.log(lb)
    cc = jnp.log(1.0 - lb) + log_sig
    g = jnp.maximum(log_lb, cc) + jnp.log(1.0 + jnp.exp(-jnp.abs(log_lb - cc)))
    cum = cum_ref[st["direction"]]
    ball_scr[slot, st["stream"]] = jnp.dot(cum, jnp.concatenate(_split3(g * LOG2_E), axis=0),
                                           preferred_element_type=F32)


def _hgrn_products(st, slot, mask_ref, sel_ref, state_ref, ball_scr, k_scr, out_scr):
    c = HGRN_CHUNK
    direction, stream = st["direction"], st["stream"]
    q, v = st["q_ref"][st["rows"], st["cols"]], st["v_ref"][st["rows"], st["cols"]]
    kk = k_scr[slot, stream]
    ball = ball_scr[slot, stream]
    b = ball[0:c]
    b_edge = b[c - 1:c] if direction == 0 else b[0:1]

    state = state_ref[stream]
    out_scr[slot, stream, 0] = _nt((q * jnp.exp2(b)).astype(BF16), state.astype(BF16))
    k_out = kk * jnp.exp2(b_edge - b)
    upd = _tn(v.astype(BF16), k_out.astype(BF16))
    r128 = lax.broadcasted_iota(jnp.int32, (LANES, LANES), 0) // HEAD_DIM
    c128 = lax.broadcasted_iota(jnp.int32, (LANES, LANES), 1) // HEAD_DIM
    state_ref[stream] = jnp.where(r128 == c128, state * jnp.exp2(b_edge) + upd, 0.0)

    att = None
    for level in range(HGRN_LEVELS):
        beta = ball[(level + 1) * c:(level + 2) * c]
        qm = mask_ref[direction, level]
        decay = jnp.exp2(-jnp.abs(b - beta))
        ql = q * decay * qm
        kl = kk * decay * (1.0 - qm)
        a = _nt(ql.astype(BF16), _by_head(kl))
        if level < HGRN_LEVELS - 1:
            a = a * mask_ref[direction, HGRN_LEVELS + level]
        att = a if att is None else att + a

    pair_cols = []
    for j in range(HGRN_SUB):
        pieces = []
        for u in range(c // HGRN_SUB):
            r = u * HGRN_SUB
            b_row = ball_scr[slot, stream, pl.ds(r + j, 1), :]
            k_row = k_scr[slot, stream, pl.ds(r + j, 1), :]
            pieces.append((q[r:r + HGRN_SUB] * k_row)
                          * jnp.exp2(jnp.minimum(b[r:r + HGRN_SUB] - b_row, 0.0)))
        pair_cols.append(jnp.concatenate(pieces, axis=0).astype(BF16))
    diag = jnp.dot(jnp.concatenate(pair_cols, axis=1), sel_ref[...], preferred_element_type=F32)
    out_scr[slot, stream, 1] = att + diag * mask_ref[direction, 2 * HGRN_LEVELS - 1]


def _hgrn_finish(st, slot, out_scr):
    v = st["v_ref"][st["rows"], st["cols"]]
    st["o_ref"][st["rows"], st["cols"]] = out_scr[slot, st["stream"], 0] + jnp.dot(
        out_scr[slot, st["stream"], 1].astype(BF16), _by_head(v), preferred_element_type=F32)


def _hgrn_body(qf_ref, vf_ref, ff_ref, qb_ref, vb_ref, fb_ref, lbf_ref, lbb_ref,
               cum_ref, mask_ref, sel_ref, of_ref, ob_ref, state_ref, ball_scr, k_scr, out_scr):
    @pl.when(pl.program_id(1) == 0)
    def _():
        state_ref[...] = jnp.zeros_like(state_ref)

    n_chunks = qf_ref.shape[0] // HGRN_CHUNK
    n_tiles = HGRN_WIDTH // LANES
    per_direction = ((qf_ref, vf_ref, ff_ref, lbf_ref, of_ref), (qb_ref, vb_ref, fb_ref, lbb_ref, ob_ref))

    def streams_of(i):
        streams = []
        for direction, (q_ref, v_ref, f_ref, lb_ref, o_ref) in enumerate(per_direction):
            chunk = i if direction == 0 else n_chunks - 1 - i
            rows = pl.ds(pl.multiple_of(chunk * HGRN_CHUNK, HGRN_CHUNK), HGRN_CHUNK)
            for tile in range(n_tiles):
                streams.append(dict(direction=direction, stream=direction * n_tiles + tile, rows=rows,
                                    cols=slice(tile * LANES, (tile + 1) * LANES), q_ref=q_ref,
                                    v_ref=v_ref, f_ref=f_ref, lb_ref=lb_ref, o_ref=o_ref))
        return streams

    for st in streams_of(0):
        _hgrn_gates(st, 0, cum_ref, ball_scr, k_scr)
    out_scr[1] = jnp.zeros(out_scr.shape[1:], out_scr.dtype)

    def step_group(group, carry):
        for k in range(HGRN_UNROLL):
            slot = k % 2
            i = HGRN_UNROLL * group + k
            for st in streams_of(jnp.maximum(i - 1, 0)):
                _hgrn_finish(st, 1 - slot, out_scr)
            for st in streams_of(jnp.minimum(i + 1, n_chunks - 1)):
                _hgrn_gates(st, 1 - slot, cum_ref, ball_scr, k_scr)
            for st in streams_of(i):
                _hgrn_products(st, slot, mask_ref, sel_ref, state_ref, ball_scr, k_scr, out_scr)
        return carry

    lax.fori_loop(0, n_chunks // HGRN_UNROLL, step_group, 0)
    for st in streams_of(n_chunks - 1):
        _hgrn_finish(st, (n_chunks - 1) % 2, out_scr)


def hgrn_scan(rest, lb_fwd, lb_bwd, *, ts=2048):
    bsz, seq, _ = rest.shape
    ts = min(ts, seq)
    nt = seq // ts
    n_streams = 2 * HGRN_WIDTH // LANES
    cum, masks, sel = _hgrn_constants()

    def fwd(c0):
        return pl.BlockSpec((None, ts, HGRN_WIDTH), lambda b, j: (b, j, c0))

    def bwd(c0):
        return pl.BlockSpec((None, ts, HGRN_WIDTH), lambda b, j: (b, nt - 1 - j, c0))

    return pl.pallas_call(
        _hgrn_body,
        grid=(bsz, nt),
        in_specs=[fwd(1), fwd(2), fwd(3), bwd(1), bwd(2), bwd(4),
                  _resident((1, HGRN_WIDTH)), _resident((1, HGRN_WIDTH)),
                  _resident(cum.shape), _resident(masks.shape), _resident(sel.shape)],
        out_specs=[fwd(0), bwd(0)],
        out_shape=[jax.ShapeDtypeStruct((bsz, seq, HGRN_WIDTH), F32)] * 2,
        scratch_shapes=[pltpu.VMEM((n_streams, LANES, LANES), F32),
                        pltpu.VMEM((2, n_streams) + cum.shape[1:2] + (LANES,), F32),
                        pltpu.VMEM((2, n_streams, HGRN_CHUNK, LANES), F32),
                        pltpu.VMEM((2, n_streams, 2, HGRN_CHUNK, LANES), F32)],
        compiler_params=_cparams(("parallel", "arbitrary")),
    )(rest, rest, rest, rest, rest, rest, lb_fwd.reshape(1, HGRN_WIDTH), lb_bwd.reshape(1, HGRN_WIDTH),
      cum, masks, sel)


def _post_body(x_ref, att_ref, u_ref, prev_ref, next_ref, of_ref, ob_ref, gate_ref,
               pw_ref, ps_ref, gain_ref, bd_ref, w_ref, g2_ref, wg_ref, wu_ref, wd_ref,
               o_ref, pad_ref, a_ref, *, seq):
    tile, n_tiles = pl.program_id(1), pl.num_programs(1)
    _pool_pad(tile, n_tiles, u_ref, prev_ref, next_ref, pad_ref)
    centred = _pool_centred(tile, 0, u_ref.shape[0], pad_ref, seq=seq)
    pooled = jnp.dot(centred.astype(BF16), pw_ref[...], preferred_element_type=F32) * ps_ref[...]
    o = of_ref[...] + ob_ref[...]
    y = o * lax.rsqrt(_head_mean_sq(o, bd_ref[...]) + EPS) * gain_ref[...]
    gate = gate_ref[...]
    rec = (y * (gate * jax.nn.sigmoid(gate))).astype(BF16)
    mixed = jnp.concatenate([att_ref[...].astype(BF16), pooled.astype(BF16), rec], axis=1)
    x = x_ref[...] + jnp.dot(mixed, w_ref[...], preferred_element_type=F32)
    o_ref[...] = _half_ffn(x, g2_ref, wg_ref, wu_ref, wd_ref, a_ref)


def post_mixer(x3, y_att, rest, o_fwd, o_bwd, pool_w, pool_scale, out_gain, w_out,
               ffn_gain, wg, wu, wd, *, tm=512):
    bsz, seq, d = x3.shape
    dff = wg.shape[1]
    tm = min(tm, seq)
    gate_tile = (rest.shape[2] - HGRN_WIDTH) // HGRN_WIDTH
    per_tile = tm // POOL_HALO
    last_halo = seq // POOL_HALO - 1

    def rowblock(width, col=0):
        return pl.BlockSpec((None, tm, width), lambda b, t: (b, t, col))

    return pl.pallas_call(
        functools.partial(_post_body, seq=seq),
        grid=(bsz, seq // tm),
        in_specs=[rowblock(d), rowblock(ATT_WIDTH), rowblock(POOL_WIDTH),
                  pl.BlockSpec((None, POOL_HALO, POOL_WIDTH),
                               lambda b, t: (b, jnp.maximum(t * per_tile - 1, 0), 0)),
                  pl.BlockSpec((None, POOL_HALO, POOL_WIDTH),
                               lambda b, t: (b, jnp.minimum((t + 1) * per_tile, last_halo), 0)),
                  rowblock(HGRN_WIDTH), rowblock(HGRN_WIDTH), rowblock(HGRN_WIDTH, gate_tile),
                  _resident((POOL_WIDTH, POOL_WIDTH)), _resident((1, POOL_WIDTH)),
                  _resident((1, HGRN_WIDTH)), _resident((HGRN_WIDTH, HGRN_WIDTH)), _resident(w_out.shape),
                  _resident((1, d)), _resident((d, dff)), _resident((d, dff)), _resident((dff, d))],
        out_specs=rowblock(d),
        out_shape=jax.ShapeDtypeStruct((bsz, seq, d), F32),
        scratch_shapes=[pltpu.VMEM((tm + 2 * POOL_HALO, POOL_WIDTH), F32), pltpu.VMEM((tm, dff), BF16)],
        compiler_params=_cparams(("parallel", "parallel")),
    )(x3, y_att, rest, rest, rest, o_fwd, o_bwd, rest,
      _pool_blockdiag(pool_w).astype(BF16), pool_scale.reshape(1, POOL_WIDTH),
      jnp.tile(out_gain, HGRN_WIDTH // HEAD_DIM).reshape(1, HGRN_WIDTH), _head_blockdiag(HGRN_WIDTH),
      w_out, ffn_gain.reshape(1, d), wg, wu, wd)


def kernel(x, ffn1_norm, ffn1_w_gate, ffn1_w_up, ffn1_w_down, mix_norm, w_in, q_norm, k_norm, rel_bias,
           pool_w, pool_scale, hgrn_lb_logits, hgrn_norm, w_out, ffn2_norm, ffn2_w_gate, ffn2_w_up,
           ffn2_w_down):
    bsz, seq, d = x.shape
    depth = w_in.shape[0]
    n = bsz * seq
    h = x.astype(F32).reshape(n, d)
    bias = attention_bias(rel_bias)
    lb_cum = jnp.cumsum(jax.nn.softmax(hgrn_lb_logits.astype(F32), axis=1), axis=1)
    lb_all = lb_cum - lb_cum[:, :1]
    for l in range(depth):
        h, qkv, rest = pre_mixer(h, seq, ffn1_norm[l], ffn1_w_gate[l].astype(BF16),
                                 ffn1_w_up[l].astype(BF16), ffn1_w_down[l].astype(BF16),
                                 mix_norm[l], w_in[l].astype(BF16), q_norm[l], k_norm[l])
        rest = rest.reshape(bsz, seq, -1)
        y_att = attention(qkv, bias)
        o_fwd, o_bwd = hgrn_scan(rest, lb_all[0, l], lb_all[1, l])
        h = post_mixer(h.reshape(bsz, seq, d), y_att, rest, o_fwd, o_bwd, pool_w[l], pool_scale[l],
                       hgrn_norm[l], w_out[l].astype(BF16), ffn2_norm[l], ffn2_w_gate[l].astype(BF16),
                       ffn2_w_up[l].astype(BF16), ffn2_w_down[l].astype(BF16)).reshape(n, d)
    return h.reshape(bsz, seq, d).astype(x.dtype)
```

```python
import functools
import math

import numpy as np
import jax
import jax.numpy as jnp
from jax import lax
from jax.experimental import pallas as pl
from jax.experimental.pallas import tpu as pltpu

F32 = jnp.float32
BF16 = jnp.bfloat16

LANES = 128
SUBLANES = 8
MXU_COLS = 256
VMEM_BYTES_V7X = 64 * 1024 * 1024
VMEM_LIMIT = VMEM_BYTES_V7X - 8 * 1024 * 1024

HEAD_DIM = 64
HEADS_PER_TILE = LANES // HEAD_DIM
ATT_WIDTH = 512
POOL_WIDTH = 256
HGRN_WIDTH = 256
POOL_WINDOWS = (2, 4, 8, 16)
DILATIONS = (1, 4, 16)
ATT_SIDE = 64
NUM_BUCKETS = 32
MAX_DISTANCE = 1024
EPS = 1e-6
NEG = -1e30
LOG2_E = math.log2(math.e)

Q_BLOCK = 128
K_BLOCK = Q_BLOCK + 2 * ATT_SIDE
ATT_UNROLL = 4
ATT_SLABS = 4
HGRN_CHUNK = 64
HGRN_SUB = SUBLANES
HGRN_LEVELS = 3
HGRN_UNROLL = 4


def _cparams(sem):
    return pltpu.CompilerParams(dimension_semantics=sem, vmem_limit_bytes=VMEM_LIMIT)


def _resident(shape):
    nd = len(shape)
    return pl.BlockSpec(shape, lambda *_: (0,) * nd, pipeline_mode=pl.Buffered(1))


def _nt(a, b):
    return lax.dot_general(a, b, (((1,), (1,)), ((), ())), preferred_element_type=F32)


def _tn(a, b):
    return lax.dot_general(a, b, (((0,), (0,)), ((), ())), preferred_element_type=F32)


def _head_blockdiag(width=LANES):
    r = np.arange(width) // HEAD_DIM
    return jnp.asarray(r[:, None] == r[None, :], dtype=BF16)


def _head_mean_sq(v, bd):
    return jnp.dot((v * v).astype(BF16), bd, preferred_element_type=F32) * (1.0 / HEAD_DIM)


def _rmsnorm_bf16(x, gain):
    return (x * lax.rsqrt(jnp.mean(x * x, axis=-1, keepdims=True) + EPS) * gain).astype(BF16)


def _half_ffn(x, g_ref, wg_ref, wu_ref, wd_ref, a_ref, between=()):
    h = _rmsnorm_bf16(x, g_ref[...])
    between = list(between)
    for c in range(wg_ref.shape[1] // MXU_COLS):
        sl = slice(c * MXU_COLS, (c + 1) * MXU_COLS)
        gate = jnp.dot(h, wg_ref[:, sl], preferred_element_type=F32)
        up = jnp.dot(h, wu_ref[:, sl], preferred_element_type=F32)
        a_ref[:, sl] = (gate * jax.nn.sigmoid(gate) * up).astype(BF16)
        thunk = between.pop(0) if between else None
        if thunk is not None:
            thunk()
    for thunk in between:
        thunk()
    return x + 0.5 * jnp.dot(a_ref[...], wd_ref[...], preferred_element_type=F32)


def _pre_body(x_ref, g1_ref, wg_ref, wu_ref, wd_ref, g_ref, w_ref, qg_ref, kg_ref, bd_ref,
              h_ref, qkv_ref, rest_ref, a_ref, perm_scr, hn_scr):
    @pl.when(pl.program_id(0) == 0)
    def _():
        hn_scr[...] = jnp.zeros_like(hn_scr)

    bd = bd_ref[...]
    wc = MXU_COLS
    slab_rows = qkv_ref.shape[1]
    n_att = 3 * ATT_WIDTH

    def to_slabs(c, z):
        for t in range(wc // LANES):
            perm_scr[c, t] = z[:, t * LANES:(t + 1) * LANES]
            for r in range(ATT_SLABS):
                lanes = slice(c * wc + t * LANES, c * wc + (t + 1) * LANES)
                qkv_ref[r, :, lanes] = perm_scr[c, t, pl.ds(r, slab_rows, stride=ATT_SLABS), :]

    projected = {}

    hn = hn_scr[...]

    def project(c):
        projected[c] = jnp.dot(hn, w_ref[:, c * wc:(c + 1) * wc], preferred_element_type=F32)

    def finish(c):
        sl = slice(c * wc, (c + 1) * wc)
        z = projected.pop(c)
        if c >= n_att // wc:
            rest_ref[:, c * wc - n_att:(c + 1) * wc - n_att] = z
            return
        if c < 2 * ATT_WIDTH // wc:
            if c < ATT_WIDTH // wc:
                gain = qg_ref[:, sl] * (HEAD_DIM ** -0.5)
            else:
                gain = kg_ref[:, c * wc - ATT_WIDTH:(c + 1) * wc - ATT_WIDTH]
            z = z * lax.rsqrt(_head_mean_sq(z, bd) + EPS) * gain
        to_slabs(c, z)

    def skewed(c):
        def thunk():
            if c > 0:
                finish(c - 1)
            if c < n_chunks:
                project(c)
        return thunk

    n_chunks = w_ref.shape[1] // wc
    thunks = [skewed(c) for c in range(n_chunks + 1)]
    lead, trail = 2, 3
    for thunk in thunks[:lead]:
        thunk()
    x = _half_ffn(x_ref[...], g1_ref, wg_ref, wu_ref, wd_ref, a_ref, between=thunks[lead:-trail])
    for thunk in thunks[-trail:]:
        thunk()
    h_ref[...] = x
    hn_scr[...] = _rmsnorm_bf16(x, g_ref[...])


def pre_mixer(x2, seq, ffn_gain, wg, wu, wd, gain, w_in, q_gain, k_gain, *, tm=512):
    n, d = x2.shape
    dff = wg.shape[1]
    cols = w_in.shape[1]
    n_att = 3 * ATT_WIDTH
    tm = min(tm, seq)
    tiles = seq // tm
    last = n // tm - 1

    def ffn_tile(i):
        return jnp.minimum(i, last)

    def proj_tile(i):
        return jnp.maximum(i - 1, 0)

    return pl.pallas_call(
        _pre_body,
        grid=(n // tm + 1,),
        in_specs=[pl.BlockSpec((tm, d), lambda i: (ffn_tile(i), 0)),
                  _resident((1, d)), _resident((d, dff)), _resident((d, dff)), _resident((dff, d)),
                  _resident((1, d)), _resident((d, cols)),
                  _resident((1, ATT_WIDTH)), _resident((1, ATT_WIDTH)), _resident((MXU_COLS, MXU_COLS))],
        out_specs=[pl.BlockSpec((tm, d), lambda i: (ffn_tile(i), 0)),
                   pl.BlockSpec((None, ATT_SLABS, tm // ATT_SLABS, n_att),
                                lambda i: (proj_tile(i) // tiles, 0, proj_tile(i) % tiles, 0)),
                   pl.BlockSpec((tm, cols - n_att), lambda i: (proj_tile(i), 0))],
        out_shape=[jax.ShapeDtypeStruct((n, d), F32),
                   jax.ShapeDtypeStruct((n // seq, ATT_SLABS, seq // ATT_SLABS, n_att), F32),
                   jax.ShapeDtypeStruct((n, cols - n_att), F32)],
        scratch_shapes=[pltpu.VMEM((tm, dff), BF16),
                        pltpu.VMEM((n_att // MXU_COLS, MXU_COLS // LANES, tm, LANES), F32),
                        pltpu.VMEM((tm, d), BF16)],
        compiler_params=_cparams(("arbitrary",)),
    )(x2, ffn_gain.reshape(1, d), wg, wu, wd, gain.reshape(1, d), w_in,
      q_gain.reshape(1, ATT_WIDTH), k_gain.reshape(1, ATT_WIDTH), _head_blockdiag(MXU_COLS))


def _t5_bucket(rel):
    half = NUM_BUCKETS // 2
    max_exact = half // 2
    base = jnp.where(rel > 0, half, 0)
    n = jnp.abs(rel)
    nf = jnp.maximum(n, 1).astype(F32)
    large = max_exact + (jnp.log(nf / max_exact) / math.log(MAX_DISTANCE / max_exact)
                         * (half - max_exact)).astype(jnp.int32)
    large = jnp.minimum(large, half - 1)
    return base + jnp.where(n < max_exact, n, large)


def _bias_buckets():
    out = []
    for dil in DILATIONS:
        qi, ki = np.arange(Q_BLOCK), np.arange(K_BLOCK)
        if dil == 1:
            qi = qi.reshape(-1, ATT_SLABS).T.reshape(-1)
            ki = ki.reshape(-1, ATT_SLABS).T.reshape(-1)
        qi, ki = jnp.asarray(qi)[:, None], jnp.asarray(ki)[None, :]
        per_var = []
        for var in range(3):
            rel = ki - var * ATT_SIDE - qi
            per_var.append(jnp.where(jnp.abs(rel) <= ATT_SIDE, _t5_bucket(rel * dil), -1))
        out.append(jnp.stack(per_var))
    return jnp.stack(out).astype(jnp.int32)


def _bias_body(tbl_ref, bk_ref, o_ref):
    hp = pl.program_id(0)
    for var in range(3):
        bk = bk_ref[var]
        for h in range(HEADS_PER_TILE):
            head = hp * HEADS_PER_TILE + h
            acc = jnp.full(bk.shape, NEG, F32)
            for b in range(NUM_BUCKETS):
                acc = jnp.where(bk == b, tbl_ref[b, head], acc)
            o_ref[var, h] = acc


def attention_bias(rel_bias):
    n_pairs = rel_bias.shape[1] // HEADS_PER_TILE
    n_g = len(DILATIONS)
    return pl.pallas_call(
        _bias_body,
        grid=(n_pairs, n_g),
        in_specs=[pl.BlockSpec(memory_space=pltpu.SMEM),
                  pl.BlockSpec((None, 3, Q_BLOCK, K_BLOCK), lambda p, g: (g, 0, 0, 0))],
        out_specs=pl.BlockSpec((None, None, 3, HEADS_PER_TILE, Q_BLOCK, K_BLOCK),
                               lambda p, g: (p, g, 0, 0, 0, 0)),
        out_shape=jax.ShapeDtypeStruct((n_pairs, n_g, 3, HEADS_PER_TILE, Q_BLOCK, K_BLOCK), F32),
        compiler_params=_cparams(("parallel", "parallel")),
    )(rel_bias.astype(F32), _bias_buckets())


def _attn_body(q_ref, k_ref, v_ref, bias_ref, o_ref, acc_ref, m_ref, l_ref, s_scr):
    slab_len = q_ref.shape[1]
    seq = ATT_SLABS * slab_len
    lo = lax.broadcasted_iota(jnp.int32, (Q_BLOCK, LANES), 1) < HEAD_DIM
    ones = jnp.ones((K_BLOCK, LANES), BF16)

    def window(n, sub_len):
        qs = n * Q_BLOCK
        ks = jnp.clip(qs - ATT_SIDE, 0, sub_len - K_BLOCK)
        var = jnp.where(n > 0, 1, 0) + jnp.where(n == sub_len // Q_BLOCK - 1, 1, 0)
        return qs, ks, var

    def pieces(dil, it, u):
        if dil == 1:
            qs, ks, var = window(it * ATT_UNROLL + u, seq)
            q0 = pl.multiple_of(qs // ATT_SLABS, Q_BLOCK // ATT_SLABS)
            k0 = pl.multiple_of(ks // ATT_SLABS, ATT_SIDE // ATT_SLABS)
            return ([(r, pl.ds(q0, Q_BLOCK // ATT_SLABS)) for r in range(ATT_SLABS)],
                    [(r, pl.ds(k0, K_BLOCK // ATT_SLABS)) for r in range(ATT_SLABS)], var)
        if dil == ATT_SLABS:
            qs, ks, var = window(it, slab_len)
            return ([(u, pl.ds(pl.multiple_of(qs, Q_BLOCK), Q_BLOCK))],
                    [(u, pl.ds(pl.multiple_of(ks, ATT_SIDE), K_BLOCK))], var)
        step = dil // ATT_SLABS
        a, n = it % step, it // step
        qs, ks, var = window(n, slab_len // step)
        return ([(u, pl.ds(a + step * qs, Q_BLOCK, stride=step))],
                [(u, pl.ds(a + step * ks, K_BLOCK, stride=step))], var)

    def load(ref, idx):
        parts = [ref[r, rows, :] for r, rows in idx]
        return parts[0] if len(parts) == 1 else jnp.concatenate(parts, axis=0)

    def store(ref, idx, val):
        n = val.shape[0] // len(idx)
        for j, (r, rows) in enumerate(idx):
            ref[r, rows, :] = val[j * n:(j + 1) * n]

    def scores(g, q_idx, k_idx, var):
        q = load(q_ref, q_idx)
        k = load(k_ref, k_idx).astype(BF16)
        qq = jnp.concatenate([jnp.where(lo, q, 0.0), jnp.where(lo, 0.0, q)], axis=0).astype(BF16)
        return _nt(qq, k) + bias_ref[g, var]

    def weighted(k_idx, s):
        vo = jnp.concatenate([load(v_ref, k_idx).astype(BF16), ones], axis=1)
        m2 = jnp.max(s, axis=1, keepdims=True)
        ol = jnp.dot(jnp.exp(s - m2).astype(BF16), vo, preferred_element_type=F32)
        top, bot = ol[:Q_BLOCK], ol[Q_BLOCK:]
        m = jnp.where(lo, jnp.broadcast_to(m2[:Q_BLOCK], (Q_BLOCK, LANES)),
                      jnp.broadcast_to(m2[Q_BLOCK:], (Q_BLOCK, LANES)))
        return (jnp.where(lo, top[:, :LANES], bot[:, :LANES]),
                jnp.where(lo, top[:, LANES:], bot[:, LANES:]), m)

    def merge(g, q_idx, o, l, m):
        if g > 0:
            m_old = load(m_ref, q_idx)
            m_new = jnp.maximum(m_old, m)
            a_old, a_cur = jnp.exp(m_old - m_new), jnp.exp(m - m_new)
            o = load(acc_ref, q_idx) * a_old + o * a_cur
            l = load(l_ref, q_idx) * a_old + l * a_cur
            m = m_new
        store(acc_ref, q_idx, o)
        store(l_ref, q_idx, l)
        store(m_ref, q_idx, m)

    n_iter = seq // Q_BLOCK // ATT_UNROLL
    for g, dil in enumerate(DILATIONS):
        def score_into(slot, it, g=g, dil=dil):
            for u in range(ATT_UNROLL):
                q_idx, k_idx, var = pieces(dil, it, u)
                s_scr[slot, u] = scores(g, q_idx, k_idx, var)

        def step_pair(pair, carry, g=g, dil=dil, score_into=score_into):
            for slot in range(2):
                it = 2 * pair + slot
                score_into(1 - slot, jnp.minimum(it + 1, n_iter - 1))
                blocks = [pieces(dil, it, u) for u in range(ATT_UNROLL)]
                done = [weighted(k_idx, s_scr[slot, u]) for u, (_, k_idx, _) in enumerate(blocks)]
                for (q_idx, _, _), res in zip(blocks, done):
                    merge(g, q_idx, *res)
            return carry

        score_into(0, 0)
        lax.fori_loop(0, n_iter // 2, step_pair, 0)

    def normalise(i, c):
        start = pl.multiple_of(i * Q_BLOCK, Q_BLOCK)
        for r in range(ATT_SLABS):
            rows = pl.ds(start, Q_BLOCK)
            o_ref[pl.ds(r + ATT_SLABS * start, Q_BLOCK, stride=ATT_SLABS), :] = (
                acc_ref[r, rows, :] / l_ref[r, rows, :])
        return c

    lax.fori_loop(0, slab_len // Q_BLOCK, normalise, 0)


def attention(qkv, bias):
    bsz, n_slab, slab_len, _ = qkv.shape
    seq = n_slab * slab_len
    n_pairs = ATT_WIDTH // LANES
    assert n_slab == ATT_SLABS == ATT_UNROLL == DILATIONS[1] and DILATIONS[2] % ATT_SLABS == 0
    assert seq % (max(DILATIONS) * Q_BLOCK) == 0 and seq // max(DILATIONS) >= K_BLOCK
    bias = bias.reshape(bias.shape[:3] + (HEADS_PER_TILE * Q_BLOCK, K_BLOCK))
    slabs = (ATT_SLABS, slab_len, LANES)

    def col(c0):
        return pl.BlockSpec((None,) + slabs, lambda b, p: (b, 0, 0, c0 + p))

    return pl.pallas_call(
        _attn_body,
        grid=(bsz, n_pairs),
        in_specs=[col(0), col(n_pairs), col(2 * n_pairs),
                  pl.BlockSpec((None,) + bias.shape[1:], lambda b, p: (p, 0, 0, 0, 0))],
        out_specs=pl.BlockSpec((None, seq, LANES), lambda b, p: (b, 0, p)),
        out_shape=jax.ShapeDtypeStruct((bsz, seq, ATT_WIDTH), F32),
        scratch_shapes=[pltpu.VMEM(slabs, F32), pltpu.VMEM(slabs, F32), pltpu.VMEM(slabs, F32),
                        pltpu.VMEM((2, ATT_UNROLL, HEADS_PER_TILE * Q_BLOCK, K_BLOCK), F32)],
        compiler_params=_cparams(("parallel", "parallel")),
    )(qkv, qkv, qkv, bias)


POOL_HALO = max(POOL_WINDOWS) // 2


POOL_MARGIN = 2 * POOL_HALO


def _pool_centred(t, n_tiles, u_ref, prev_ref, next_ref, inv_cnt_ref, pad_ref):
    assert POOL_WINDOWS == tuple(2 << k for k in range(len(POOL_WINDOWS)))
    ts = u_ref.shape[0]
    m = POOL_MARGIN
    n_ext = ts + 2 * POOL_HALO
    edge = jnp.zeros((POOL_HALO, POOL_WIDTH), F32)
    for k in range(len(POOL_WINDOWS) - 1):
        pad_ref[k, pl.ds(0, POOL_HALO), :] = edge
        pad_ref[k, pl.ds(m + ts + POOL_HALO, POOL_HALO), :] = edge
    pad_ref[0, pl.ds(m - POOL_HALO, POOL_HALO), :] = jnp.where(t > 0, prev_ref[...], 0.0)
    pad_ref[0, pl.ds(m, ts), :] = u_ref[...]
    pad_ref[0, pl.ds(m + ts, POOL_HALO), :] = jnp.where(t < n_tiles - 1, next_ref[...], 0.0)

    sums = []
    for k, win in enumerate(POOL_WINDOWS):
        lo, hi = (1, 0) if k == 0 else (win // 4, -(win // 4))
        if k < len(POOL_WINDOWS) - 1:
            pad_ref[k + 1, pl.ds(POOL_HALO, n_ext), :] = (pad_ref[k, pl.ds(POOL_HALO - lo, n_ext), :]
                                                         + pad_ref[k, pl.ds(POOL_HALO - hi, n_ext), :])
            sums.append(pad_ref[k + 1, pl.ds(m, ts), :])
        else:
            sums.append(pad_ref[k, pl.ds(m - lo, ts), :] + pad_ref[k, pl.ds(m - hi, ts), :])

    lane_win = lax.broadcasted_iota(jnp.int32, (1, POOL_WIDTH), 1) // HEAD_DIM
    total = sums[0]
    for gi in range(1, len(POOL_WINDOWS)):
        total = jnp.where(lane_win == gi, sums[gi], total)
    return total * inv_cnt_ref[...] - pad_ref[0, pl.ds(m, ts), :]


def _pool_inverse_counts(seq):
    pos = jnp.arange(seq, dtype=jnp.int32)[:, None]
    win = jnp.repeat(jnp.asarray(POOL_WINDOWS, jnp.int32), HEAD_DIM)[None, :]
    cnt = jnp.minimum(pos + win - win // 2, seq) - jnp.maximum(pos - win // 2, 0)
    return 1.0 / cnt.astype(F32)


def _pool_blockdiag(w_pool):
    n_g = w_pool.shape[0]
    eye = jnp.eye(n_g, dtype=w_pool.dtype)
    return jnp.einsum('gcd,gh->gchd', w_pool, eye).reshape(n_g * HEAD_DIM, n_g * HEAD_DIM)


def _hgrn_constants():
    c = HGRN_CHUNK
    t = np.arange(c)
    s_col = np.tile(t, HEADS_PER_TILE)
    cum, masks = [], []
    for direction in range(2):
        tri = (t[None, :] <= t[:, None]) if direction == 0 else (t[None, :] >= t[:, None])
        tri = tri.astype(np.float32)
        mats, qms, bms = [tri], [], []
        for level in range(HGRN_LEVELS):
            half = HGRN_SUB << level
            start = (t // (2 * half)) * (2 * half)
            boundary = start + (half - 1 if direction == 0 else half)
            mats.append(tri[boundary])
            q_right = (t // half) % 2 == 1
            qm = q_right if direction == 0 else ~q_right
            qms.append(np.broadcast_to(qm[:, None], (c, LANES)))
            bms.append(t[:, None] // (2 * half) == s_col[None, :] // (2 * half))
        same_sub = t[:, None] // HGRN_SUB == s_col[None, :] // HGRN_SUB
        causal = (s_col[None, :] <= t[:, None]) if direction == 0 else (s_col[None, :] >= t[:, None])
        cum.append(np.tile(np.concatenate(mats, axis=0), (1, 3)))
        masks.append(np.stack(qms + bms[:HGRN_LEVELS - 1] + [same_sub & causal]).astype(np.float32))
    lane_head = np.arange(LANES) // HEAD_DIM
    sel = np.concatenate([(lane_head[:, None] == lane_head[None, :]) & (s_col[None, :] % HGRN_SUB == j)
                          for j in range(HGRN_SUB)], axis=0).astype(np.float32)
    return (jnp.asarray(np.stack(cum), BF16), jnp.asarray(np.stack(masks), F32), jnp.asarray(sel, BF16))


def _split3(x):
    x1 = x.astype(BF16)
    r1 = x - x1.astype(F32)
    x2 = r1.astype(BF16)
    x3 = (r1 - x2.astype(F32)).astype(BF16)
    return x1, x2, x3


def _by_head(x):
    lo = lax.broadcasted_iota(jnp.int32, x.shape, 1) < HEAD_DIM
    return jnp.concatenate([jnp.where(lo, x, 0.0), jnp.where(lo, 0.0, x)], axis=0).astype(BF16)


def _hgrn_gates(st, slot, cum_ref, ball_scr, k_scr):
    fl = st["f_ref"][st["rows"], st["cols"]]
    lb = st["lb_ref"][:, st["cols"]]
    e = jnp.exp(-jnp.abs(fl))
    one_plus_e = 1.0 + e
    log_sig = jnp.minimum(fl, 0.0) - jnp.log(one_plus_e)
    rcp = 1.0 / one_plus_e
    k_scr[slot, st["stream"]] = (1.0 - lb) * jnp.where(fl >= 0, e * rcp, rcp)
    log_lb = jnp.log(lb)
    cc = jnp.log(1.0 - lb) + log_sig
    g = jnp.maximum(log_lb, cc) + jnp.log(1.0 + jnp.exp(-jnp.abs(log_lb - cc)))
    cum = cum_ref[st["direction"]]
    ball_scr[slot, st["stream"]] = jnp.dot(cum, jnp.concatenate(_split3(g * LOG2_E), axis=0),
                                           preferred_element_type=F32)


def _hgrn_products(st, slot, mask_ref, sel_ref, state_ref, ball_scr, k_scr, out_scr):
    c = HGRN_CHUNK
    direction, stream = st["direction"], st["stream"]
    q, v = st["q_ref"][st["rows"], st["cols"]], st["v_ref"][st["rows"], st["cols"]]
    kk = k_scr[slot, stream]
    ball = ball_scr[slot, stream]
    b = ball[0:c]
    b_edge = b[c - 1:c] if direction == 0 else b[0:1]

    state = state_ref[stream]
    out_scr[slot, stream, 0] = _nt((q * jnp.exp2(b)).astype(BF16), state.astype(BF16))
    k_out = kk * jnp.exp2(b_edge - b)
    upd = _tn(v.astype(BF16), k_out.astype(BF16))
    r128 = lax.broadcasted_iota(jnp.int32, (LANES, LANES), 0) // HEAD_DIM
    c128 = lax.broadcasted_iota(jnp.int32, (LANES, LANES), 1) // HEAD_DIM
    state_ref[stream] = jnp.where(r128 == c128, state * jnp.exp2(b_edge) + upd, 0.0)

    att = None
    for level in range(HGRN_LEVELS):
        beta = ball[(level + 1) * c:(level + 2) * c]
        qm = mask_ref[direction, level]
        decay = jnp.exp2(-jnp.abs(b - beta))
        ql = q * decay * qm
        kl = kk * decay * (1.0 - qm)
        a = _nt(ql.astype(BF16), _by_head(kl))
        if level < HGRN_LEVELS - 1:
            a = a * mask_ref[direction, HGRN_LEVELS + level]
        att = a if att is None else att + a

    pair_cols = []
    for j in range(HGRN_SUB):
        pieces = []
        for u in range(c // HGRN_SUB):
            r = u * HGRN_SUB
            b_row = ball_scr[slot, stream, pl.ds(r + j, 1), :]
            k_row = k_scr[slot, stream, pl.ds(r + j, 1), :]
            pieces.append((q[r:r + HGRN_SUB] * k_row)
                          * jnp.exp2(jnp.minimum(b[r:r + HGRN_SUB] - b_row, 0.0)))
        pair_cols.append(jnp.concatenate(pieces, axis=0).astype(BF16))
    diag = jnp.dot(jnp.concatenate(pair_cols, axis=1), sel_ref[...], preferred_element_type=F32)
    out_scr[slot, stream, 1] = att + diag * mask_ref[direction, 2 * HGRN_LEVELS - 1]


def _hgrn_finish(st, slot, out_scr):
    v = st["v_ref"][st["rows"], st["cols"]]
    st["o_ref"][st["rows"], st["cols"]] = out_scr[slot, st["stream"], 0] + jnp.dot(
        out_scr[slot, st["stream"], 1].astype(BF16), _by_head(v), preferred_element_type=F32)


def _hgrn_body(qf_ref, vf_ref, ff_ref, qb_ref, vb_ref, fb_ref, lbf_ref, lbb_ref,
               cum_ref, mask_ref, sel_ref, of_ref, ob_ref, state_ref, ball_scr, k_scr, out_scr):
    @pl.when(pl.program_id(1) == 0)
    def _():
        state_ref[...] = jnp.zeros_like(state_ref)

    n_chunks = qf_ref.shape[0] // HGRN_CHUNK
    n_tiles = HGRN_WIDTH // LANES
    per_direction = ((qf_ref, vf_ref, ff_ref, lbf_ref, of_ref), (qb_ref, vb_ref, fb_ref, lbb_ref, ob_ref))

    def streams_of(i):
        streams = []
        for direction, (q_ref, v_ref, f_ref, lb_ref, o_ref) in enumerate(per_direction):
            chunk = i if direction == 0 else n_chunks - 1 - i
            rows = pl.ds(pl.multiple_of(chunk * HGRN_CHUNK, HGRN_CHUNK), HGRN_CHUNK)
            for tile in range(n_tiles):
                streams.append(dict(direction=direction, stream=direction * n_tiles + tile, rows=rows,
                                    cols=slice(tile * LANES, (tile + 1) * LANES), q_ref=q_ref,
                                    v_ref=v_ref, f_ref=f_ref, lb_ref=lb_ref, o_ref=o_ref))
        return streams

    for st in streams_of(0):
        _hgrn_gates(st, 0, cum_ref, ball_scr, k_scr)
    out_scr[1] = jnp.zeros(out_scr.shape[1:], out_scr.dtype)

    def step_group(group, carry):
        for k in range(HGRN_UNROLL):
            slot = k % 2
            i = HGRN_UNROLL * group + k
            for st in streams_of(jnp.maximum(i - 1, 0)):
                _hgrn_finish(st, 1 - slot, out_scr)
            for st in streams_of(jnp.minimum(i + 1, n_chunks - 1)):
                _hgrn_gates(st, 1 - slot, cum_ref, ball_scr, k_scr)
            for st in streams_of(i):
                _hgrn_products(st, slot, mask_ref, sel_ref, state_ref, ball_scr, k_scr, out_scr)
        return carry

    lax.fori_loop(0, n_chunks // HGRN_UNROLL, step_group, 0)
    for st in streams_of(n_chunks - 1):
        _hgrn_finish(st, (n_chunks - 1) % 2, out_scr)


def hgrn_scan(rest, lb_fwd, lb_bwd, *, ts=2048):
    bsz, seq, _ = rest.shape
    ts = min(ts, seq)
    nt = seq // ts
    n_streams = 2 * HGRN_WIDTH // LANES
    cum, masks, sel = _hgrn_constants()

    def fwd(c0):
        return pl.BlockSpec((None, ts, HGRN_WIDTH), lambda b, j: (b, j, c0))

    def bwd(c0):
        return pl.BlockSpec((None, ts, HGRN_WIDTH), lambda b, j: (b, nt - 1 - j, c0))

    return pl.pallas_call(
        _hgrn_body,
        grid=(bsz, nt),
        in_specs=[fwd(1), fwd(2), fwd(3), bwd(1), bwd(2), bwd(4),
                  _resident((1, HGRN_WIDTH)), _resident((1, HGRN_WIDTH)),
                  _resident(cum.shape), _resident(masks.shape), _resident(sel.shape)],
        out_specs=[fwd(0), bwd(0)],
        out_shape=[jax.ShapeDtypeStruct((bsz, seq, HGRN_WIDTH), F32)] * 2,
        scratch_shapes=[pltpu.VMEM((n_streams, LANES, LANES), F32),
                        pltpu.VMEM((2, n_streams) + cum.shape[1:2] + (LANES,), F32),
                        pltpu.VMEM((2, n_streams, HGRN_CHUNK, LANES), F32),
                        pltpu.VMEM((2, n_streams, 2, HGRN_CHUNK, LANES), F32)],
        compiler_params=_cparams(("parallel", "arbitrary")),
    )(rest, rest, rest, rest, rest, rest, lb_fwd.reshape(1, HGRN_WIDTH), lb_bwd.reshape(1, HGRN_WIDTH),
      cum, masks, sel)


def _post_body(x_ref, att_ref, u_ref, prev_ref, next_ref, inv_cnt_ref, of_ref, ob_ref, gate_ref,
               pw_ref, ps_ref, gain_ref, bd_ref, w_ref, g2_ref, wg_ref, wu_ref, wd_ref,
               o_ref, pad_ref, a_ref):
    tile, n_tiles = pl.program_id(1), pl.num_programs(1)
    centred = _pool_centred(tile, n_tiles, u_ref, prev_ref, next_ref, inv_cnt_ref, pad_ref)
    pooled = jnp.dot(centred.astype(BF16), pw_ref[...], preferred_element_type=F32) * ps_ref[...]
    o = of_ref[...] + ob_ref[...]
    y = o * lax.rsqrt(_head_mean_sq(o, bd_ref[...]) + EPS) * gain_ref[...]
    gate = gate_ref[...]
    rec = (y * (gate * jax.nn.sigmoid(gate))).astype(BF16)
    mixed = jnp.concatenate([att_ref[...].astype(BF16), pooled.astype(BF16), rec], axis=1)
    x = x_ref[...] + jnp.dot(mixed, w_ref[...], preferred_element_type=F32)
    o_ref[...] = _half_ffn(x, g2_ref, wg_ref, wu_ref, wd_ref, a_ref)


def post_mixer(x3, y_att, rest, o_fwd, o_bwd, pool_w, pool_scale, out_gain, w_out,
               ffn_gain, wg, wu, wd, *, tm=512):
    bsz, seq, d = x3.shape
    dff = wg.shape[1]
    tm = min(tm, seq)
    gate_tile = (rest.shape[2] - HGRN_WIDTH) // HGRN_WIDTH
    per_tile = tm // POOL_HALO
    last_halo = seq // POOL_HALO - 1

    def rowblock(width, col=0):
        return pl.BlockSpec((None, tm, width), lambda b, t: (b, t, col))

    return pl.pallas_call(
        _post_body,
        grid=(bsz, seq // tm),
        in_specs=[rowblock(d), rowblock(ATT_WIDTH), rowblock(POOL_WIDTH),
                  pl.BlockSpec((None, POOL_HALO, POOL_WIDTH),
                               lambda b, t: (b, jnp.maximum(t * per_tile - 1, 0), 0)),
                  pl.BlockSpec((None, POOL_HALO, POOL_WIDTH),
                               lambda b, t: (b, jnp.minimum((t + 1) * per_tile, last_halo), 0)),
                  pl.BlockSpec((tm, POOL_WIDTH), lambda b, t: (t, 0)),
                  rowblock(HGRN_WIDTH), rowblock(HGRN_WIDTH), rowblock(HGRN_WIDTH, gate_tile),
                  _resident((POOL_WIDTH, POOL_WIDTH)), _resident((1, POOL_WIDTH)),
                  _resident((1, HGRN_WIDTH)), _resident((HGRN_WIDTH, HGRN_WIDTH)), _resident(w_out.shape),
                  _resident((1, d)), _resident((d, dff)), _resident((d, dff)), _resident((dff, d))],
        out_specs=rowblock(d),
        out_shape=jax.ShapeDtypeStruct((bsz, seq, d), F32),
        scratch_shapes=[pltpu.VMEM((len(POOL_WINDOWS), tm + 2 * POOL_MARGIN, POOL_WIDTH), F32),
                        pltpu.VMEM((tm, dff), BF16)],
        compiler_params=_cparams(("parallel", "parallel")),
    )(x3, y_att, rest, rest, rest, _pool_inverse_counts(seq), o_fwd, o_bwd, rest,
      _pool_blockdiag(pool_w).astype(BF16), pool_scale.reshape(1, POOL_WIDTH),
      jnp.tile(out_gain, HGRN_WIDTH // HEAD_DIM).reshape(1, HGRN_WIDTH), _head_blockdiag(HGRN_WIDTH),
      w_out, ffn_gain.reshape(1, d), wg, wu, wd)


def kernel(x, ffn1_norm, ffn1_w_gate, ffn1_w_up, ffn1_w_down, mix_norm, w_in, q_norm, k_norm, rel_bias,
           pool_w, pool_scale, hgrn_lb_logits, hgrn_norm, w_out, ffn2_norm, ffn2_w_gate, ffn2_w_up,
           ffn2_w_down):
    bsz, seq, d = x.shape
    depth = w_in.shape[0]
    n = bsz * seq
    h = x.astype(F32).reshape(n, d)
    bias = attention_bias(rel_bias)
    lb_cum = jnp.cumsum(jax.nn.softmax(hgrn_lb_logits.astype(F32), axis=1), axis=1)
    lb_all = lb_cum - lb_cum[:, :1]
    for l in range(depth):
        h, qkv, rest = pre_mixer(h, seq, ffn1_norm[l], ffn1_w_gate[l].astype(BF16),
                                 ffn1_w_up[l].astype(BF16), ffn1_w_down[l].astype(BF16),
                                 mix_norm[l], w_in[l].astype(BF16), q_norm[l], k_norm[l])
        rest = rest.reshape(bsz, seq, -1)
        y_att = attention(qkv, bias)
        o_fwd, o_bwd = hgrn_scan(rest, lb_all[0, l], lb_all[1, l])
        h = post_mixer(h.reshape(bsz, seq, d), y_att, rest, o_fwd, o_bwd, pool_w[l], pool_scale[l],
                       hgrn_norm[l], w_out[l].astype(BF16), ffn2_norm[l], ffn2_w_gate[l].astype(BF16),
                       ffn2_w_up[l].astype(BF16), ffn2_w_down[l].astype(BF16)).reshape(n, d)
    return h.reshape(bsz, seq, d).astype(x.dtype)
```

```python
import functools
import math

import numpy as np
import jax
import jax.numpy as jnp
from jax import lax
from jax.experimental import pallas as pl
from jax.experimental.pallas import tpu as pltpu

F32 = jnp.float32
BF16 = jnp.bfloat16

LANES = 128
SUBLANES = 8
MXU_COLS = 256
VMEM_BYTES_V7X = 64 * 1024 * 1024
VMEM_LIMIT = VMEM_BYTES_V7X - 8 * 1024 * 1024

HEAD_DIM = 64
HEADS_PER_TILE = LANES // HEAD_DIM
ATT_WIDTH = 512
POOL_WIDTH = 256
HGRN_WIDTH = 256
POOL_WINDOWS = (2, 4, 8, 16)
DILATIONS = (1, 4, 16)
ATT_SIDE = 64
NUM_BUCKETS = 32
MAX_DISTANCE = 1024
EPS = 1e-6
NEG = -1e30
LOG2_E = math.log2(math.e)

Q_BLOCK = 128
K_BLOCK = Q_BLOCK + 2 * ATT_SIDE
ATT_UNROLL = 4
ATT_SLABS = 4
HGRN_CHUNK = 64
HGRN_SUB = SUBLANES
HGRN_LEVELS = 3
HGRN_UNROLL = 4
HGRN_SAFE_GATE = 8.0


def _cparams(sem):
    return pltpu.CompilerParams(dimension_semantics=sem, vmem_limit_bytes=VMEM_LIMIT)


def _resident(shape):
    nd = len(shape)
    return pl.BlockSpec(shape, lambda *_: (0,) * nd, pipeline_mode=pl.Buffered(1))


def _nt(a, b):
    return lax.dot_general(a, b, (((1,), (1,)), ((), ())), preferred_element_type=F32)


def _tn(a, b):
    return lax.dot_general(a, b, (((0,), (0,)), ((), ())), preferred_element_type=F32)


def _head_blockdiag(width=LANES):
    r = np.arange(width) // HEAD_DIM
    return jnp.asarray(r[:, None] == r[None, :], dtype=BF16)


def _head_mean_sq(v, bd):
    return jnp.dot((v * v).astype(BF16), bd, preferred_element_type=F32) * (1.0 / HEAD_DIM)


def _rmsnorm_bf16(x, gain):
    return (x * lax.rsqrt(jnp.mean(x * x, axis=-1, keepdims=True) + EPS) * gain).astype(BF16)


def _half_ffn(x, g_ref, wg_ref, wu_ref, wd_ref, a_ref, between=()):
    h = _rmsnorm_bf16(x, g_ref[...])
    between = list(between)
    for c in range(wg_ref.shape[1] // MXU_COLS):
        sl = slice(c * MXU_COLS, (c + 1) * MXU_COLS)
        gate = jnp.dot(h, wg_ref[:, sl], preferred_element_type=F32)
        up = jnp.dot(h, wu_ref[:, sl], preferred_element_type=F32)
        a_ref[:, sl] = (gate * jax.nn.sigmoid(gate) * up).astype(BF16)
        thunk = between.pop(0) if between else None
        if thunk is not None:
            thunk()
    for thunk in between:
        thunk()
    return x + 0.5 * jnp.dot(a_ref[...], wd_ref[...], preferred_element_type=F32)


def _pre_body(x_ref, g1_ref, wg_ref, wu_ref, wd_ref, g_ref, w_ref, qg_ref, kg_ref, bd_ref,
              h_ref, qkv_ref, rest_ref, a_ref, perm_scr, hn_scr):
    @pl.when(pl.program_id(0) == 0)
    def _():
        hn_scr[...] = jnp.zeros_like(hn_scr)

    bd = bd_ref[...]
    wc = MXU_COLS
    slab_rows = qkv_ref.shape[1]
    n_att = 3 * ATT_WIDTH

    def to_slabs(c, z):
        for t in range(wc // LANES):
            perm_scr[c, t] = z[:, t * LANES:(t + 1) * LANES]
            for r in range(ATT_SLABS):
                lanes = slice(c * wc + t * LANES, c * wc + (t + 1) * LANES)
                qkv_ref[r, :, lanes] = perm_scr[c, t, pl.ds(r, slab_rows, stride=ATT_SLABS), :]

    projected = {}

    hn = hn_scr[...]

    def project(c):
        projected[c] = jnp.dot(hn, w_ref[:, c * wc:(c + 1) * wc], preferred_element_type=F32)

    def finish(c):
        sl = slice(c * wc, (c + 1) * wc)
        z = projected.pop(c)
        if c >= n_att // wc:
            rest_ref[:, c * wc - n_att:(c + 1) * wc - n_att] = z
            return
        if c < 2 * ATT_WIDTH // wc:
            if c < ATT_WIDTH // wc:
                gain = qg_ref[:, sl] * (HEAD_DIM ** -0.5)
            else:
                gain = kg_ref[:, c * wc - ATT_WIDTH:(c + 1) * wc - ATT_WIDTH]
            z = z * lax.rsqrt(_head_mean_sq(z, bd) + EPS) * gain
        to_slabs(c, z)

    def skewed(c):
        def thunk():
            if c > 0:
                finish(c - 1)
            if c < n_chunks:
                project(c)
        return thunk

    n_chunks = w_ref.shape[1] // wc
    thunks = [skewed(c) for c in range(n_chunks + 1)]
    lead, trail = 2, 3
    for thunk in thunks[:lead]:
        thunk()
    x = _half_ffn(x_ref[...], g1_ref, wg_ref, wu_ref, wd_ref, a_ref, between=thunks[lead:-trail])
    for thunk in thunks[-trail:]:
        thunk()
    h_ref[...] = x
    hn_scr[...] = _rmsnorm_bf16(x, g_ref[...])


def pre_mixer(x2, seq, ffn_gain, wg, wu, wd, gain, w_in, q_gain, k_gain, *, tm=512):
    n, d = x2.shape
    dff = wg.shape[1]
    cols = w_in.shape[1]
    n_att = 3 * ATT_WIDTH
    tm = min(tm, seq)
    tiles = seq // tm
    last = n // tm - 1

    def ffn_tile(i):
        return jnp.minimum(i, last)

    def proj_tile(i):
        return jnp.maximum(i - 1, 0)

    return pl.pallas_call(
        _pre_body,
        grid=(n // tm + 1,),
        in_specs=[pl.BlockSpec((tm, d), lambda i: (ffn_tile(i), 0)),
                  _resident((1, d)), _resident((d, dff)), _resident((d, dff)), _resident((dff, d)),
                  _resident((1, d)), _resident((d, cols)),
                  _resident((1, ATT_WIDTH)), _resident((1, ATT_WIDTH)), _resident((MXU_COLS, MXU_COLS))],
        out_specs=[pl.BlockSpec((tm, d), lambda i: (ffn_tile(i), 0)),
                   pl.BlockSpec((None, ATT_SLABS, tm // ATT_SLABS, n_att),
                                lambda i: (proj_tile(i) // tiles, 0, proj_tile(i) % tiles, 0)),
                   pl.BlockSpec((tm, cols - n_att), lambda i: (proj_tile(i), 0))],
        out_shape=[jax.ShapeDtypeStruct((n, d), F32),
                   jax.ShapeDtypeStruct((n // seq, ATT_SLABS, seq // ATT_SLABS, n_att), F32),
                   jax.ShapeDtypeStruct((n, cols - n_att), F32)],
        scratch_shapes=[pltpu.VMEM((tm, dff), BF16),
                        pltpu.VMEM((n_att // MXU_COLS, MXU_COLS // LANES, tm, LANES), F32),
                        pltpu.VMEM((tm, d), BF16)],
        compiler_params=_cparams(("arbitrary",)),
    )(x2, ffn_gain.reshape(1, d), wg, wu, wd, gain.reshape(1, d), w_in,
      q_gain.reshape(1, ATT_WIDTH), k_gain.reshape(1, ATT_WIDTH), _head_blockdiag(MXU_COLS))


def _t5_bucket(rel):
    half = NUM_BUCKETS // 2
    max_exact = half // 2
    base = jnp.where(rel > 0, half, 0)
    n = jnp.abs(rel)
    nf = jnp.maximum(n, 1).astype(F32)
    large = max_exact + (jnp.log(nf / max_exact) / math.log(MAX_DISTANCE / max_exact)
                         * (half - max_exact)).astype(jnp.int32)
    large = jnp.minimum(large, half - 1)
    return base + jnp.where(n < max_exact, n, large)


def _bias_buckets():
    out = []
    for dil in DILATIONS:
        qi, ki = np.arange(Q_BLOCK), np.arange(K_BLOCK)
        if dil == 1:
            qi = qi.reshape(-1, ATT_SLABS).T.reshape(-1)
            ki = ki.reshape(-1, ATT_SLABS).T.reshape(-1)
        qi, ki = jnp.asarray(qi)[:, None], jnp.asarray(ki)[None, :]
        per_var = []
        for var in range(3):
            rel = ki - var * ATT_SIDE - qi
            per_var.append(jnp.where(jnp.abs(rel) <= ATT_SIDE, _t5_bucket(rel * dil), -1))
        out.append(jnp.stack(per_var))
    return jnp.stack(out).astype(jnp.int32)


def _bias_body(tbl_ref, bk_ref, o_ref):
    hp = pl.program_id(0)
    for var in range(3):
        bk = bk_ref[var]
        for h in range(HEADS_PER_TILE):
            head = hp * HEADS_PER_TILE + h
            acc = jnp.full(bk.shape, NEG, F32)
            for b in range(NUM_BUCKETS):
                acc = jnp.where(bk == b, tbl_ref[b, head], acc)
            o_ref[var, h] = acc


def attention_bias(rel_bias):
    n_pairs = rel_bias.shape[1] // HEADS_PER_TILE
    n_g = len(DILATIONS)
    return pl.pallas_call(
        _bias_body,
        grid=(n_pairs, n_g),
        in_specs=[pl.BlockSpec(memory_space=pltpu.SMEM),
                  pl.BlockSpec((None, 3, Q_BLOCK, K_BLOCK), lambda p, g: (g, 0, 0, 0))],
        out_specs=pl.BlockSpec((None, None, 3, HEADS_PER_TILE, Q_BLOCK, K_BLOCK),
                               lambda p, g: (p, g, 0, 0, 0, 0)),
        out_shape=jax.ShapeDtypeStruct((n_pairs, n_g, 3, HEADS_PER_TILE, Q_BLOCK, K_BLOCK), F32),
        compiler_params=_cparams(("parallel", "parallel")),
    )(rel_bias.astype(F32), _bias_buckets())


def _attn_body(q_ref, k_ref, v_ref, bias_ref, o_ref, acc_ref, m_ref, l_ref, s_scr):
    slab_len = q_ref.shape[1]
    seq = ATT_SLABS * slab_len
    lo = lax.broadcasted_iota(jnp.int32, (Q_BLOCK, LANES), 1) < HEAD_DIM
    ones = jnp.ones((K_BLOCK, LANES), BF16)

    def window(n, sub_len):
        qs = n * Q_BLOCK
        ks = jnp.clip(qs - ATT_SIDE, 0, sub_len - K_BLOCK)
        var = jnp.where(n > 0, 1, 0) + jnp.where(n == sub_len // Q_BLOCK - 1, 1, 0)
        return qs, ks, var

    def pieces(dil, it, u):
        if dil == 1:
            qs, ks, var = window(it * ATT_UNROLL + u, seq)
            q0 = pl.multiple_of(qs // ATT_SLABS, Q_BLOCK // ATT_SLABS)
            k0 = pl.multiple_of(ks // ATT_SLABS, ATT_SIDE // ATT_SLABS)
            return ([(r, pl.ds(q0, Q_BLOCK // ATT_SLABS)) for r in range(ATT_SLABS)],
                    [(r, pl.ds(k0, K_BLOCK // ATT_SLABS)) for r in range(ATT_SLABS)], var)
        if dil == ATT_SLABS:
            qs, ks, var = window(it, slab_len)
            return ([(u, pl.ds(pl.multiple_of(qs, Q_BLOCK), Q_BLOCK))],
                    [(u, pl.ds(pl.multiple_of(ks, ATT_SIDE), K_BLOCK))], var)
        step = dil // ATT_SLABS
        a, n = it % step, it // step
        qs, ks, var = window(n, slab_len // step)
        return ([(u, pl.ds(a + step * qs, Q_BLOCK, stride=step))],
                [(u, pl.ds(a + step * ks, K_BLOCK, stride=step))], var)

    def load(ref, idx):
        parts = [ref[r, rows, :] for r, rows in idx]
        return parts[0] if len(parts) == 1 else jnp.concatenate(parts, axis=0)

    def store(ref, idx, val):
        n = val.shape[0] // len(idx)
        for j, (r, rows) in enumerate(idx):
            ref[r, rows, :] = val[j * n:(j + 1) * n]

    def scores(g, q_idx, k_idx, var):
        q = load(q_ref, q_idx)
        k = load(k_ref, k_idx).astype(BF16)
        qq = jnp.concatenate([jnp.where(lo, q, 0.0), jnp.where(lo, 0.0, q)], axis=0).astype(BF16)
        return _nt(qq, k) + bias_ref[g, var]

    def weighted(k_idx, s):
        vo = jnp.concatenate([load(v_ref, k_idx).astype(BF16), ones], axis=1)
        m2 = jnp.max(s, axis=1, keepdims=True)
        ol = jnp.dot(jnp.exp(s - m2).astype(BF16), vo, preferred_element_type=F32)
        top, bot = ol[:Q_BLOCK], ol[Q_BLOCK:]
        m = jnp.where(lo, jnp.broadcast_to(m2[:Q_BLOCK], (Q_BLOCK, LANES)),
                      jnp.broadcast_to(m2[Q_BLOCK:], (Q_BLOCK, LANES)))
        return (jnp.where(lo, top[:, :LANES], bot[:, :LANES]),
                jnp.where(lo, top[:, LANES:], bot[:, LANES:]), m)

    def merge(g, q_idx, o, l, m):
        if g > 0:
            m_old = load(m_ref, q_idx)
            m_new = jnp.maximum(m_old, m)
            a_old, a_cur = jnp.exp(m_old - m_new), jnp.exp(m - m_new)
            o = load(acc_ref, q_idx) * a_old + o * a_cur
            l = load(l_ref, q_idx) * a_old + l * a_cur
            m = m_new
        store(acc_ref, q_idx, o)
        store(l_ref, q_idx, l)
        store(m_ref, q_idx, m)

    n_iter = seq // Q_BLOCK // ATT_UNROLL
    for g, dil in enumerate(DILATIONS):
        def score_into(slot, it, g=g, dil=dil):
            for u in range(ATT_UNROLL):
                q_idx, k_idx, var = pieces(dil, it, u)
                s_scr[slot, u] = scores(g, q_idx, k_idx, var)

        def step_pair(pair, carry, g=g, dil=dil, score_into=score_into):
            for slot in range(2):
                it = 2 * pair + slot
                score_into(1 - slot, jnp.minimum(it + 1, n_iter - 1))
                blocks = [pieces(dil, it, u) for u in range(ATT_UNROLL)]
                done = [weighted(k_idx, s_scr[slot, u]) for u, (_, k_idx, _) in enumerate(blocks)]
                for (q_idx, _, _), res in zip(blocks, done):
                    merge(g, q_idx, *res)
            return carry

        score_into(0, 0)
        lax.fori_loop(0, n_iter // 2, step_pair, 0)

    def normalise(i, c):
        start = pl.multiple_of(i * Q_BLOCK, Q_BLOCK)
        for r in range(ATT_SLABS):
            rows = pl.ds(start, Q_BLOCK)
            o_ref[pl.ds(r + ATT_SLABS * start, Q_BLOCK, stride=ATT_SLABS), :] = (
                acc_ref[r, rows, :] / l_ref[r, rows, :])
        return c

    lax.fori_loop(0, slab_len // Q_BLOCK, normalise, 0)


def attention(qkv, bias):
    bsz, n_slab, slab_len, _ = qkv.shape
    seq = n_slab * slab_len
    n_pairs = ATT_WIDTH // LANES
    assert n_slab == ATT_SLABS == ATT_UNROLL == DILATIONS[1] and DILATIONS[2] % ATT_SLABS == 0
    assert seq % (max(DILATIONS) * Q_BLOCK) == 0 and seq // max(DILATIONS) >= K_BLOCK
    bias = bias.reshape(bias.shape[:3] + (HEADS_PER_TILE * Q_BLOCK, K_BLOCK))
    slabs = (ATT_SLABS, slab_len, LANES)

    def col(c0):
        return pl.BlockSpec((None,) + slabs, lambda b, p: (b, 0, 0, c0 + p))

    return pl.pallas_call(
        _attn_body,
        grid=(bsz, n_pairs),
        in_specs=[col(0), col(n_pairs), col(2 * n_pairs),
                  pl.BlockSpec((None,) + bias.shape[1:], lambda b, p: (p, 0, 0, 0, 0))],
        out_specs=pl.BlockSpec((None, seq, LANES), lambda b, p: (b, 0, p)),
        out_shape=jax.ShapeDtypeStruct((bsz, seq, ATT_WIDTH), F32),
        scratch_shapes=[pltpu.VMEM(slabs, F32), pltpu.VMEM(slabs, F32), pltpu.VMEM(slabs, F32),
                        pltpu.VMEM((2, ATT_UNROLL, HEADS_PER_TILE * Q_BLOCK, K_BLOCK), F32)],
        compiler_params=_cparams(("parallel", "parallel")),
    )(qkv, qkv, qkv, bias)


POOL_HALO = max(POOL_WINDOWS) // 2


POOL_MARGIN = 2 * POOL_HALO


def _pool_centred(t, n_tiles, u_ref, prev_ref, next_ref, inv_cnt_ref, pad_ref):
    assert POOL_WINDOWS == tuple(2 << k for k in range(len(POOL_WINDOWS)))
    ts = u_ref.shape[0]
    m = POOL_MARGIN
    n_ext = ts + 2 * POOL_HALO
    edge = jnp.zeros((POOL_HALO, POOL_WIDTH), F32)
    for k in range(len(POOL_WINDOWS) - 1):
        pad_ref[k, pl.ds(0, POOL_HALO), :] = edge
        pad_ref[k, pl.ds(m + ts + POOL_HALO, POOL_HALO), :] = edge
    pad_ref[0, pl.ds(m - POOL_HALO, POOL_HALO), :] = jnp.where(t > 0, prev_ref[...], 0.0)
    pad_ref[0, pl.ds(m, ts), :] = u_ref[...]
    pad_ref[0, pl.ds(m + ts, POOL_HALO), :] = jnp.where(t < n_tiles - 1, next_ref[...], 0.0)

    sums = []
    for k, win in enumerate(POOL_WINDOWS):
        lo, hi = (1, 0) if k == 0 else (win // 4, -(win // 4))
        if k < len(POOL_WINDOWS) - 1:
            pad_ref[k + 1, pl.ds(POOL_HALO, n_ext), :] = (pad_ref[k, pl.ds(POOL_HALO - lo, n_ext), :]
                                                         + pad_ref[k, pl.ds(POOL_HALO - hi, n_ext), :])
            sums.append(pad_ref[k + 1, pl.ds(m, ts), :])
        else:
            sums.append(pad_ref[k, pl.ds(m - lo, ts), :] + pad_ref[k, pl.ds(m - hi, ts), :])

    lane_win = lax.broadcasted_iota(jnp.int32, (1, POOL_WIDTH), 1) // HEAD_DIM
    total = sums[0]
    for gi in range(1, len(POOL_WINDOWS)):
        total = jnp.where(lane_win == gi, sums[gi], total)
    return total * inv_cnt_ref[...] - pad_ref[0, pl.ds(m, ts), :]


def _pool_inverse_counts(seq):
    pos = jnp.arange(seq, dtype=jnp.int32)[:, None]
    win = jnp.repeat(jnp.asarray(POOL_WINDOWS, jnp.int32), HEAD_DIM)[None, :]
    cnt = jnp.minimum(pos + win - win // 2, seq) - jnp.maximum(pos - win // 2, 0)
    return 1.0 / cnt.astype(F32)


def _pool_blockdiag(w_pool):
    n_g = w_pool.shape[0]
    eye = jnp.eye(n_g, dtype=w_pool.dtype)
    return jnp.einsum('gcd,gh->gchd', w_pool, eye).reshape(n_g * HEAD_DIM, n_g * HEAD_DIM)


def _hgrn_constants():
    c = HGRN_CHUNK
    t = np.arange(c)
    s_col = np.tile(t, HEADS_PER_TILE)
    cum, masks = [], []
    for direction in range(2):
        tri = (t[None, :] <= t[:, None]) if direction == 0 else (t[None, :] >= t[:, None])
        tri = tri.astype(np.float32)
        mats, qms, bms = [tri], [], []
        for level in range(HGRN_LEVELS):
            half = HGRN_SUB << level
            start = (t // (2 * half)) * (2 * half)
            boundary = start + (half - 1 if direction == 0 else half)
            mats.append(tri[boundary])
            q_right = (t // half) % 2 == 1
            qm = q_right if direction == 0 else ~q_right
            qms.append(np.broadcast_to(qm[:, None], (c, LANES)))
            bms.append(t[:, None] // (2 * half) == s_col[None, :] // (2 * half))
        mats.append(tri[(t // HGRN_SUB) * HGRN_SUB + (0 if direction == 0 else HGRN_SUB - 1)])
        same_sub = t[:, None] // HGRN_SUB == s_col[None, :] // HGRN_SUB
        causal = (s_col[None, :] <= t[:, None]) if direction == 0 else (s_col[None, :] >= t[:, None])
        cum.append(np.tile(np.concatenate(mats, axis=0), (1, 3)))
        masks.append(np.stack(qms + bms[:HGRN_LEVELS - 1] + [same_sub & causal]).astype(np.float32))
    lane_head = np.arange(LANES) // HEAD_DIM
    sel = np.concatenate([(lane_head[:, None] == lane_head[None, :]) & (s_col[None, :] % HGRN_SUB == j)
                          for j in range(HGRN_SUB)], axis=0).astype(np.float32)
    return (jnp.asarray(np.stack(cum), BF16), jnp.asarray(np.stack(masks), F32), jnp.asarray(sel, BF16))


def _split3(x):
    x1 = x.astype(BF16)
    r1 = x - x1.astype(F32)
    x2 = r1.astype(BF16)
    x3 = (r1 - x2.astype(F32)).astype(BF16)
    return x1, x2, x3


def _by_head(x):
    lo = lax.broadcasted_iota(jnp.int32, x.shape, 1) < HEAD_DIM
    return jnp.concatenate([jnp.where(lo, x, 0.0), jnp.where(lo, 0.0, x)], axis=0).astype(BF16)


def _hgrn_gates(st, slot, cum_ref, ball_scr, k_scr):
    fl = st["f_ref"][st["rows"], st["cols"]]
    lb = st["lb_ref"][:, st["cols"]]
    e = jnp.exp(-jnp.abs(fl))
    one_plus_e = 1.0 + e
    log_sig = jnp.minimum(fl, 0.0) - jnp.log(one_plus_e)
    rcp = 1.0 / one_plus_e
    k_scr[slot, st["stream"]] = (1.0 - lb) * jnp.where(fl >= 0, e * rcp, rcp)
    log_lb = jnp.log(lb)
    cc = jnp.log(1.0 - lb) + log_sig
    g = jnp.maximum(log_lb, cc) + jnp.log(1.0 + jnp.exp(-jnp.abs(log_lb - cc)))
    cum = cum_ref[st["direction"]]
    ball_scr[slot, st["stream"]] = jnp.dot(cum, jnp.concatenate(_split3(g * LOG2_E), axis=0),
                                           preferred_element_type=F32)


def _hgrn_products(st, slot, mask_ref, sel_ref, state_ref, ball_scr, k_scr, out_scr, *, pairwise):
    c = HGRN_CHUNK
    direction, stream = st["direction"], st["stream"]
    q, v = st["q_ref"][st["rows"], st["cols"]], st["v_ref"][st["rows"], st["cols"]]
    kk = k_scr[slot, stream]
    ball = ball_scr[slot, stream]
    b = ball[0:c]
    b_edge = b[c - 1:c] if direction == 0 else b[0:1]

    state = state_ref[stream]
    out_scr[slot, stream, 0] = _nt((q * jnp.exp2(b)).astype(BF16), state.astype(BF16))
    k_out = kk * jnp.exp2(b_edge - b)
    upd = _tn(v.astype(BF16), k_out.astype(BF16))
    r128 = lax.broadcasted_iota(jnp.int32, (LANES, LANES), 0) // HEAD_DIM
    c128 = lax.broadcasted_iota(jnp.int32, (LANES, LANES), 1) // HEAD_DIM
    state_ref[stream] = jnp.where(r128 == c128, state * jnp.exp2(b_edge) + upd, 0.0)

    att = None
    for level in range(HGRN_LEVELS):
        beta = ball[(level + 1) * c:(level + 2) * c]
        qm = mask_ref[direction, level]
        decay = jnp.exp2(-jnp.abs(b - beta))
        ql = q * decay * qm
        kl = kk * decay * (1.0 - qm)
        a = _nt(ql.astype(BF16), _by_head(kl))
        if level < HGRN_LEVELS - 1:
            a = a * mask_ref[direction, HGRN_LEVELS + level]
        att = a if att is None else att + a

    if pairwise:
        pair_cols = []
        for j in range(HGRN_SUB):
            pieces = []
            for u in range(c // HGRN_SUB):
                r = u * HGRN_SUB
                b_row = ball_scr[slot, stream, pl.ds(r + j, 1), :]
                k_row = k_scr[slot, stream, pl.ds(r + j, 1), :]
                pieces.append((q[r:r + HGRN_SUB] * k_row)
                              * jnp.exp2(jnp.minimum(b[r:r + HGRN_SUB] - b_row, 0.0)))
            pair_cols.append(jnp.concatenate(pieces, axis=0).astype(BF16))
        diag = jnp.dot(jnp.concatenate(pair_cols, axis=1), sel_ref[...], preferred_element_type=F32)
    else:
        ref = ball[(HGRN_LEVELS + 1) * c:(HGRN_LEVELS + 2) * c]
        diag = _nt((q * jnp.exp2(b - ref)).astype(BF16), _by_head(kk * jnp.exp2(ref - b)))
    out_scr[slot, stream, 1] = att + diag * mask_ref[direction, 2 * HGRN_LEVELS - 1]


def _hgrn_finish(st, slot, out_scr):
    v = st["v_ref"][st["rows"], st["cols"]]
    st["o_ref"][st["rows"], st["cols"]] = out_scr[slot, st["stream"], 0] + jnp.dot(
        out_scr[slot, st["stream"], 1].astype(BF16), _by_head(v), preferred_element_type=F32)


def _hgrn_body(qf_ref, vf_ref, ff_ref, qb_ref, vb_ref, fb_ref, lbf_ref, lbb_ref,
               cum_ref, mask_ref, sel_ref, of_ref, ob_ref, state_ref, ball_scr, k_scr, out_scr):
    @pl.when(pl.program_id(1) == 0)
    def _():
        state_ref[...] = jnp.zeros_like(state_ref)

    n_chunks = qf_ref.shape[0] // HGRN_CHUNK
    n_tiles = HGRN_WIDTH // LANES
    per_direction = ((qf_ref, vf_ref, ff_ref, lbf_ref, of_ref), (qb_ref, vb_ref, fb_ref, lbb_ref, ob_ref))

    def streams_of(i):
        streams = []
        for direction, (q_ref, v_ref, f_ref, lb_ref, o_ref) in enumerate(per_direction):
            chunk = i if direction == 0 else n_chunks - 1 - i
            rows = pl.ds(pl.multiple_of(chunk * HGRN_CHUNK, HGRN_CHUNK), HGRN_CHUNK)
            for tile in range(n_tiles):
                streams.append(dict(direction=direction, stream=direction * n_tiles + tile, rows=rows,
                                    cols=slice(tile * LANES, (tile + 1) * LANES), q_ref=q_ref,
                                    v_ref=v_ref, f_ref=f_ref, lb_ref=lb_ref, o_ref=o_ref))
        return streams

    for st in streams_of(0):
        _hgrn_gates(st, 0, cum_ref, ball_scr, k_scr)
    out_scr[1] = jnp.zeros(out_scr.shape[1:], out_scr.dtype)

    def step_group(group, carry, *, pairwise):
        for k in range(HGRN_UNROLL):
            slot = k % 2
            i = HGRN_UNROLL * group + k
            for st in streams_of(jnp.maximum(i - 1, 0)):
                _hgrn_finish(st, 1 - slot, out_scr)
            for st in streams_of(jnp.minimum(i + 1, n_chunks - 1)):
                _hgrn_gates(st, 1 - slot, cum_ref, ball_scr, k_scr)
            for st in streams_of(i):
                _hgrn_products(st, slot, mask_ref, sel_ref, state_ref, ball_scr, k_scr, out_scr,
                               pairwise=pairwise)
        return carry

    factorisable = jnp.maximum(jnp.max(-ff_ref[...]), jnp.max(-fb_ref[...])) <= HGRN_SAFE_GATE
    for pairwise in (False, True):
        @pl.when(factorisable != pairwise)
        def _(pairwise=pairwise):
            lax.fori_loop(0, n_chunks // HGRN_UNROLL, functools.partial(step_group, pairwise=pairwise), 0)

    for st in streams_of(n_chunks - 1):
        _hgrn_finish(st, (n_chunks - 1) % 2, out_scr)


def hgrn_scan(rest, lb_fwd, lb_bwd, *, ts=2048):
    bsz, seq, _ = rest.shape
    ts = min(ts, seq)
    nt = seq // ts
    n_streams = 2 * HGRN_WIDTH // LANES
    cum, masks, sel = _hgrn_constants()

    def fwd(c0):
        return pl.BlockSpec((None, ts, HGRN_WIDTH), lambda b, j: (b, j, c0))

    def bwd(c0):
        return pl.BlockSpec((None, ts, HGRN_WIDTH), lambda b, j: (b, nt - 1 - j, c0))

    return pl.pallas_call(
        _hgrn_body,
        grid=(bsz, nt),
        in_specs=[fwd(1), fwd(2), fwd(3), bwd(1), bwd(2), bwd(4),
                  _resident((1, HGRN_WIDTH)), _resident((1, HGRN_WIDTH)),
                  _resident(cum.shape), _resident(masks.shape), _resident(sel.shape)],
        out_specs=[fwd(0), bwd(0)],
        out_shape=[jax.ShapeDtypeStruct((bsz, seq, HGRN_WIDTH), F32)] * 2,
        scratch_shapes=[pltpu.VMEM((n_streams, LANES, LANES), F32),
                        pltpu.VMEM((2, n_streams) + cum.shape[1:2] + (LANES,), F32),
                        pltpu.VMEM((2, n_streams, HGRN_CHUNK, LANES), F32),
                        pltpu.VMEM((2, n_streams, 2, HGRN_CHUNK, LANES), F32)],
        compiler_params=_cparams(("parallel", "arbitrary")),
    )(rest, rest, rest, rest, rest, rest, lb_fwd.reshape(1, HGRN_WIDTH), lb_bwd.reshape(1, HGRN_WIDTH),
      cum, masks, sel)


def _post_body(x_ref, att_ref, u_ref, prev_ref, next_ref, inv_cnt_ref, of_ref, ob_ref, gate_ref,
               pw_ref, ps_ref, gain_ref, bd_ref, w_ref, g2_ref, wg_ref, wu_ref, wd_ref,
               o_ref, pad_ref, a_ref):
    tile, n_tiles = pl.program_id(1), pl.num_programs(1)
    centred = _pool_centred(tile, n_tiles, u_ref, prev_ref, next_ref, inv_cnt_ref, pad_ref)
    pooled = jnp.dot(centred.astype(BF16), pw_ref[...], preferred_element_type=F32) * ps_ref[...]
    o = of_ref[...] + ob_ref[...]
    y = o * lax.rsqrt(_head_mean_sq(o, bd_ref[...]) + EPS) * gain_ref[...]
    gate = gate_ref[...]
    rec = (y * (gate * jax.nn.sigmoid(gate))).astype(BF16)
    mixed = jnp.concatenate([att_ref[...].astype(BF16), pooled.astype(BF16), rec], axis=1)
    x = x_ref[...] + jnp.dot(mixed, w_ref[...], preferred_element_type=F32)
    o_ref[...] = _half_ffn(x, g2_ref, wg_ref, wu_ref, wd_ref, a_ref)


def post_mixer(x3, y_att, rest, o_fwd, o_bwd, pool_w, pool_scale, out_gain, w_out,
               ffn_gain, wg, wu, wd, *, tm=512):
    bsz, seq, d = x3.shape
    dff = wg.shape[1]
    tm = min(tm, seq)
    gate_tile = (rest.shape[2] - HGRN_WIDTH) // HGRN_WIDTH
    per_tile = tm // POOL_HALO
    last_halo = seq // POOL_HALO - 1

    def rowblock(width, col=0):
        return pl.BlockSpec((None, tm, width), lambda b, t: (b, t, col))

    return pl.pallas_call(
        _post_body,
        grid=(bsz, seq // tm),
        in_specs=[rowblock(d), rowblock(ATT_WIDTH), rowblock(POOL_WIDTH),
                  pl.BlockSpec((None, POOL_HALO, POOL_WIDTH),
                               lambda b, t: (b, jnp.maximum(t * per_tile - 1, 0), 0)),
                  pl.BlockSpec((None, POOL_HALO, POOL_WIDTH),
                               lambda b, t: (b, jnp.minimum((t + 1) * per_tile, last_halo), 0)),
                  pl.BlockSpec((tm, POOL_WIDTH), lambda b, t: (t, 0)),
                  rowblock(HGRN_WIDTH), rowblock(HGRN_WIDTH), rowblock(HGRN_WIDTH, gate_tile),
                  _resident((POOL_WIDTH, POOL_WIDTH)), _resident((1, POOL_WIDTH)),
                  _resident((1, HGRN_WIDTH)), _resident((HGRN_WIDTH, HGRN_WIDTH)), _resident(w_out.shape),
                  _resident((1, d)), _resident((d, dff)), _resident((d, dff)), _resident((dff, d))],
        out_specs=rowblock(d),
        out_shape=jax.ShapeDtypeStruct((bsz, seq, d), F32),
        scratch_shapes=[pltpu.VMEM((len(POOL_WINDOWS), tm + 2 * POOL_MARGIN, POOL_WIDTH), F32),
                        pltpu.VMEM((tm, dff), BF16)],
        compiler_params=_cparams(("parallel", "parallel")),
    )(x3, y_att, rest, rest, rest, _pool_inverse_counts(seq), o_fwd, o_bwd, rest,
      _pool_blockdiag(pool_w).astype(BF16), pool_scale.reshape(1, POOL_WIDTH),
      jnp.tile(out_gain, HGRN_WIDTH // HEAD_DIM).reshape(1, HGRN_WIDTH), _head_blockdiag(HGRN_WIDTH),
      w_out, ffn_gain.reshape(1, d), wg, wu, wd)


def kernel(x, ffn1_norm, ffn1_w_gate, ffn1_w_up, ffn1_w_down, mix_norm, w_in, q_norm, k_norm, rel_bias,
           pool_w, pool_scale, hgrn_lb_logits, hgrn_norm, w_out, ffn2_norm, ffn2_w_gate, ffn2_w_up,
           ffn2_w_down):
    bsz, seq, d = x.shape
    depth = w_in.shape[0]
    n = bsz * seq
    h = x.astype(F32).reshape(n, d)
    bias = attention_bias(rel_bias)
    lb_cum = jnp.cumsum(jax.nn.softmax(hgrn_lb_logits.astype(F32), axis=1), axis=1)
    lb_all = lb_cum - lb_cum[:, :1]
    for l in range(depth):
        h, qkv, rest = pre_mixer(h, seq, ffn1_norm[l], ffn1_w_gate[l].astype(BF16),
                                 ffn1_w_up[l].astype(BF16), ffn1_w_down[l].astype(BF16),
                                 mix_norm[l], w_in[l].astype(BF16), q_norm[l], k_norm[l])
        rest = rest.reshape(bsz, seq, -1)
        y_att = attention(qkv, bias)
        o_fwd, o_bwd = hgrn_scan(rest, lb_all[0, l], lb_all[1, l])
        h = post_mixer(h.reshape(bsz, seq, d), y_att, rest, o_fwd, o_bwd, pool_w[l], pool_scale[l],
                       hgrn_norm[l], w_out[l].astype(BF16), ffn2_norm[l], ffn2_w_gate[l].astype(BF16),
                       ffn2_w_up[l].astype(BF16), ffn2_w_down[l].astype(BF16)).reshape(n, d)
    return h.reshape(bsz, seq, d).astype(x.dtype)
```

```python
import functools
import math

import numpy as np
import jax
import jax.numpy as jnp
from jax import lax
from jax.experimental import pallas as pl
from jax.experimental.pallas import tpu as pltpu

F32 = jnp.float32
BF16 = jnp.bfloat16

LANES = 128
SUBLANES = 8
MXU_COLS = 256
VMEM_BYTES_V7X = 64 * 1024 * 1024
VMEM_LIMIT = VMEM_BYTES_V7X - 8 * 1024 * 1024

HEAD_DIM = 64
HEADS_PER_TILE = LANES // HEAD_DIM
ATT_WIDTH = 512
POOL_WIDTH = 256
HGRN_WIDTH = 256
POOL_WINDOWS = (2, 4, 8, 16)
DILATIONS = (1, 4, 16)
ATT_SIDE = 64
NUM_BUCKETS = 32
MAX_DISTANCE = 1024
EPS = 1e-6
NEG = -1e30
LOG2_E = math.log2(math.e)

Q_BLOCK = 128
K_BLOCK = Q_BLOCK + 2 * ATT_SIDE
ATT_UNROLL = 4
ATT_SLABS = 4
HGRN_CHUNK = 64
HGRN_SUB = SUBLANES
HGRN_LEVELS = 3
HGRN_UNROLL = 4
HGRN_BATCH = 2
HGRN_SAFE_GATE = 8.0


def _cparams(sem):
    return pltpu.CompilerParams(dimension_semantics=sem, vmem_limit_bytes=VMEM_LIMIT)


def _resident(shape):
    nd = len(shape)
    return pl.BlockSpec(shape, lambda *_: (0,) * nd, pipeline_mode=pl.Buffered(1))


def _nt(a, b):
    return lax.dot_general(a, b, (((1,), (1,)), ((), ())), preferred_element_type=F32)


def _tn(a, b):
    return lax.dot_general(a, b, (((0,), (0,)), ((), ())), preferred_element_type=F32)


def _head_blockdiag(width=LANES):
    r = np.arange(width) // HEAD_DIM
    return jnp.asarray(r[:, None] == r[None, :], dtype=BF16)


def _head_mean_sq(v, bd):
    return jnp.dot((v * v).astype(BF16), bd, preferred_element_type=F32) * (1.0 / HEAD_DIM)


def _rmsnorm_bf16(x, gain):
    return (x * lax.rsqrt(jnp.mean(x * x, axis=-1, keepdims=True) + EPS) * gain).astype(BF16)


def _half_ffn(x, g_ref, wg_ref, wu_ref, wd_ref, a_ref, between=()):
    h = _rmsnorm_bf16(x, g_ref[...])
    between = list(between)
    for c in range(wg_ref.shape[1] // MXU_COLS):
        sl = slice(c * MXU_COLS, (c + 1) * MXU_COLS)
        gate = jnp.dot(h, wg_ref[:, sl], preferred_element_type=F32)
        up = jnp.dot(h, wu_ref[:, sl], preferred_element_type=F32)
        a_ref[:, sl] = (gate * jax.nn.sigmoid(gate) * up).astype(BF16)
        thunk = between.pop(0) if between else None
        if thunk is not None:
            thunk()
    for thunk in between:
        thunk()
    return x + 0.5 * jnp.dot(a_ref[...], wd_ref[...], preferred_element_type=F32)


def _pre_body(x_ref, g1_ref, wg_ref, wu_ref, wd_ref, g_ref, w_ref, qg_ref, kg_ref, bd_ref,
              h_ref, qkv_ref, rest_ref, a_ref, perm_scr, hn_scr):
    @pl.when(pl.program_id(0) == 0)
    def _():
        hn_scr[...] = jnp.zeros_like(hn_scr)

    bd = bd_ref[...]
    wc = MXU_COLS
    slab_rows = qkv_ref.shape[1]
    n_att = 3 * ATT_WIDTH

    def to_slabs(c, z):
        for t in range(wc // LANES):
            perm_scr[c, t] = z[:, t * LANES:(t + 1) * LANES]
            for r in range(ATT_SLABS):
                lanes = slice(c * wc + t * LANES, c * wc + (t + 1) * LANES)
                qkv_ref[r, :, lanes] = perm_scr[c, t, pl.ds(r, slab_rows, stride=ATT_SLABS), :]

    projected = {}

    hn = hn_scr[...]

    def project(c):
        projected[c] = jnp.dot(hn, w_ref[:, c * wc:(c + 1) * wc], preferred_element_type=F32)

    def finish(c):
        sl = slice(c * wc, (c + 1) * wc)
        z = projected.pop(c)
        if c >= n_att // wc:
            rest_ref[:, c * wc - n_att:(c + 1) * wc - n_att] = z
            return
        if c < 2 * ATT_WIDTH // wc:
            if c < ATT_WIDTH // wc:
                gain = qg_ref[:, sl] * (HEAD_DIM ** -0.5)
            else:
                gain = kg_ref[:, c * wc - ATT_WIDTH:(c + 1) * wc - ATT_WIDTH]
            z = z * lax.rsqrt(_head_mean_sq(z, bd) + EPS) * gain
        to_slabs(c, z)

    def skewed(c):
        def thunk():
            if c > 0:
                finish(c - 1)
            if c < n_chunks:
                project(c)
        return thunk

    n_chunks = w_ref.shape[1] // wc
    thunks = [skewed(c) for c in range(n_chunks + 1)]
    lead, trail = 2, 3
    for thunk in thunks[:lead]:
        thunk()
    x = _half_ffn(x_ref[...], g1_ref, wg_ref, wu_ref, wd_ref, a_ref, between=thunks[lead:-trail])
    for thunk in thunks[-trail:]:
        thunk()
    h_ref[...] = x
    hn_scr[...] = _rmsnorm_bf16(x, g_ref[...])


def pre_mixer(x2, seq, ffn_gain, wg, wu, wd, gain, w_in, q_gain, k_gain, *, tm=512):
    n, d = x2.shape
    dff = wg.shape[1]
    cols = w_in.shape[1]
    n_att = 3 * ATT_WIDTH
    tm = min(tm, seq)
    tiles = seq // tm
    last = n // tm - 1

    def ffn_tile(i):
        return jnp.minimum(i, last)

    def proj_tile(i):
        return jnp.maximum(i - 1, 0)

    return pl.pallas_call(
        _pre_body,
        grid=(n // tm + 1,),
        in_specs=[pl.BlockSpec((tm, d), lambda i: (ffn_tile(i), 0)),
                  _resident((1, d)), _resident((d, dff)), _resident((d, dff)), _resident((dff, d)),
                  _resident((1, d)), _resident((d, cols)),
                  _resident((1, ATT_WIDTH)), _resident((1, ATT_WIDTH)), _resident((MXU_COLS, MXU_COLS))],
        out_specs=[pl.BlockSpec((tm, d), lambda i: (ffn_tile(i), 0)),
                   pl.BlockSpec((None, ATT_SLABS, tm // ATT_SLABS, n_att),
                                lambda i: (proj_tile(i) // tiles, 0, proj_tile(i) % tiles, 0)),
                   pl.BlockSpec((tm, cols - n_att), lambda i: (proj_tile(i), 0))],
        out_shape=[jax.ShapeDtypeStruct((n, d), F32),
                   jax.ShapeDtypeStruct((n // seq, ATT_SLABS, seq // ATT_SLABS, n_att), F32),
                   jax.ShapeDtypeStruct((n, cols - n_att), F32)],
        scratch_shapes=[pltpu.VMEM((tm, dff), BF16),
                        pltpu.VMEM((n_att // MXU_COLS, MXU_COLS // LANES, tm, LANES), F32),
                        pltpu.VMEM((tm, d), BF16)],
        compiler_params=_cparams(("arbitrary",)),
    )(x2, ffn_gain.reshape(1, d), wg, wu, wd, gain.reshape(1, d), w_in,
      q_gain.reshape(1, ATT_WIDTH), k_gain.reshape(1, ATT_WIDTH), _head_blockdiag(MXU_COLS))


def _t5_bucket(rel):
    half = NUM_BUCKETS // 2
    max_exact = half // 2
    base = jnp.where(rel > 0, half, 0)
    n = jnp.abs(rel)
    nf = jnp.maximum(n, 1).astype(F32)
    large = max_exact + (jnp.log(nf / max_exact) / math.log(MAX_DISTANCE / max_exact)
                         * (half - max_exact)).astype(jnp.int32)
    large = jnp.minimum(large, half - 1)
    return base + jnp.where(n < max_exact, n, large)


def _bias_buckets():
    out = []
    for dil in DILATIONS:
        qi, ki = np.arange(Q_BLOCK), np.arange(K_BLOCK)
        if dil == 1:
            qi = qi.reshape(-1, ATT_SLABS).T.reshape(-1)
            ki = ki.reshape(-1, ATT_SLABS).T.reshape(-1)
        qi, ki = jnp.asarray(qi)[:, None], jnp.asarray(ki)[None, :]
        per_var = []
        for var in range(3):
            rel = ki - var * ATT_SIDE - qi
            per_var.append(jnp.where(jnp.abs(rel) <= ATT_SIDE, _t5_bucket(rel * dil), -1))
        out.append(jnp.stack(per_var))
    return jnp.stack(out).astype(jnp.int32)


def _bias_body(tbl_ref, bk_ref, o_ref):
    hp = pl.program_id(0)
    for var in range(3):
        bk = bk_ref[var]
        for h in range(HEADS_PER_TILE):
            head = hp * HEADS_PER_TILE + h
            acc = jnp.full(bk.shape, NEG, F32)
            for b in range(NUM_BUCKETS):
                acc = jnp.where(bk == b, tbl_ref[b, head], acc)
            o_ref[var, h] = acc


def attention_bias(rel_bias):
    n_pairs = rel_bias.shape[1] // HEADS_PER_TILE
    n_g = len(DILATIONS)
    return pl.pallas_call(
        _bias_body,
        grid=(n_pairs, n_g),
        in_specs=[pl.BlockSpec(memory_space=pltpu.SMEM),
                  pl.BlockSpec((None, 3, Q_BLOCK, K_BLOCK), lambda p, g: (g, 0, 0, 0))],
        out_specs=pl.BlockSpec((None, None, 3, HEADS_PER_TILE, Q_BLOCK, K_BLOCK),
                               lambda p, g: (p, g, 0, 0, 0, 0)),
        out_shape=jax.ShapeDtypeStruct((n_pairs, n_g, 3, HEADS_PER_TILE, Q_BLOCK, K_BLOCK), F32),
        compiler_params=_cparams(("parallel", "parallel")),
    )(rel_bias.astype(F32), _bias_buckets())


def _attn_body(q_ref, k_ref, v_ref, bias_ref, o_ref, acc_ref, m_ref, l_ref, s_scr):
    slab_len = q_ref.shape[1]
    seq = ATT_SLABS * slab_len
    lo = lax.broadcasted_iota(jnp.int32, (Q_BLOCK, LANES), 1) < HEAD_DIM
    ones = jnp.ones((K_BLOCK, LANES), BF16)

    def window(n, sub_len):
        qs = n * Q_BLOCK
        ks = jnp.clip(qs - ATT_SIDE, 0, sub_len - K_BLOCK)
        var = jnp.where(n > 0, 1, 0) + jnp.where(n == sub_len // Q_BLOCK - 1, 1, 0)
        return qs, ks, var

    def pieces(dil, it, u):
        if dil == 1:
            qs, ks, var = window(it * ATT_UNROLL + u, seq)
            q0 = pl.multiple_of(qs // ATT_SLABS, Q_BLOCK // ATT_SLABS)
            k0 = pl.multiple_of(ks // ATT_SLABS, ATT_SIDE // ATT_SLABS)
            return ([(r, pl.ds(q0, Q_BLOCK // ATT_SLABS)) for r in range(ATT_SLABS)],
                    [(r, pl.ds(k0, K_BLOCK // ATT_SLABS)) for r in range(ATT_SLABS)], var)
        if dil == ATT_SLABS:
            qs, ks, var = window(it, slab_len)
            return ([(u, pl.ds(pl.multiple_of(qs, Q_BLOCK), Q_BLOCK))],
                    [(u, pl.ds(pl.multiple_of(ks, ATT_SIDE), K_BLOCK))], var)
        step = dil // ATT_SLABS
        a, n = it % step, it // step
        qs, ks, var = window(n, slab_len // step)
        return ([(u, pl.ds(a + step * qs, Q_BLOCK, stride=step))],
                [(u, pl.ds(a + step * ks, K_BLOCK, stride=step))], var)

    def load(ref, idx):
        parts = [ref[r, rows, :] for r, rows in idx]
        return parts[0] if len(parts) == 1 else jnp.concatenate(parts, axis=0)

    def store(ref, idx, val):
        n = val.shape[0] // len(idx)
        for j, (r, rows) in enumerate(idx):
            ref[r, rows, :] = val[j * n:(j + 1) * n]

    def scores(g, q_idx, k_idx, var):
        q = load(q_ref, q_idx)
        k = load(k_ref, k_idx).astype(BF16)
        qq = jnp.concatenate([jnp.where(lo, q, 0.0), jnp.where(lo, 0.0, q)], axis=0).astype(BF16)
        return _nt(qq, k) + bias_ref[g, var]

    def weighted(k_idx, s):
        vo = jnp.concatenate([load(v_ref, k_idx).astype(BF16), ones], axis=1)
        m2 = jnp.max(s, axis=1, keepdims=True)
        ol = jnp.dot(jnp.exp(s - m2).astype(BF16), vo, preferred_element_type=F32)
        top, bot = ol[:Q_BLOCK], ol[Q_BLOCK:]
        m = jnp.where(lo, jnp.broadcast_to(m2[:Q_BLOCK], (Q_BLOCK, LANES)),
                      jnp.broadcast_to(m2[Q_BLOCK:], (Q_BLOCK, LANES)))
        return (jnp.where(lo, top[:, :LANES], bot[:, :LANES]),
                jnp.where(lo, top[:, LANES:], bot[:, LANES:]), m)

    def merge(g, q_idx, o, l, m):
        if g > 0:
            m_old = load(m_ref, q_idx)
            m_new = jnp.maximum(m_old, m)
            a_old, a_cur = jnp.exp(m_old - m_new), jnp.exp(m - m_new)
            o = load(acc_ref, q_idx) * a_old + o * a_cur
            l = load(l_ref, q_idx) * a_old + l * a_cur
            m = m_new
        store(acc_ref, q_idx, o)
        store(l_ref, q_idx, l)
        store(m_ref, q_idx, m)

    n_iter = seq // Q_BLOCK // ATT_UNROLL
    for g, dil in enumerate(DILATIONS):
        def score_into(slot, it, g=g, dil=dil):
            for u in range(ATT_UNROLL):
                q_idx, k_idx, var = pieces(dil, it, u)
                s_scr[slot, u] = scores(g, q_idx, k_idx, var)

        def step_pair(pair, carry, g=g, dil=dil, score_into=score_into):
            for slot in range(2):
                it = 2 * pair + slot
                score_into(1 - slot, jnp.minimum(it + 1, n_iter - 1))
                blocks = [pieces(dil, it, u) for u in range(ATT_UNROLL)]
                done = [weighted(k_idx, s_scr[slot, u]) for u, (_, k_idx, _) in enumerate(blocks)]
                for (q_idx, _, _), res in zip(blocks, done):
                    merge(g, q_idx, *res)
            return carry

        score_into(0, 0)
        lax.fori_loop(0, n_iter // 2, step_pair, 0)

    def normalise(i, c):
        start = pl.multiple_of(i * Q_BLOCK, Q_BLOCK)
        for r in range(ATT_SLABS):
            rows = pl.ds(start, Q_BLOCK)
            o_ref[pl.ds(r + ATT_SLABS * start, Q_BLOCK, stride=ATT_SLABS), :] = (
                acc_ref[r, rows, :] / l_ref[r, rows, :])
        return c

    lax.fori_loop(0, slab_len // Q_BLOCK, normalise, 0)


def attention(qkv, bias):
    bsz, n_slab, slab_len, _ = qkv.shape
    seq = n_slab * slab_len
    n_pairs = ATT_WIDTH // LANES
    assert n_slab == ATT_SLABS == ATT_UNROLL == DILATIONS[1] and DILATIONS[2] % ATT_SLABS == 0
    assert seq % (max(DILATIONS) * Q_BLOCK) == 0 and seq // max(DILATIONS) >= K_BLOCK
    bias = bias.reshape(bias.shape[:3] + (HEADS_PER_TILE * Q_BLOCK, K_BLOCK))
    slabs = (ATT_SLABS, slab_len, LANES)

    def col(c0):
        return pl.BlockSpec((None,) + slabs, lambda b, p: (b, 0, 0, c0 + p))

    return pl.pallas_call(
        _attn_body,
        grid=(bsz, n_pairs),
        in_specs=[col(0), col(n_pairs), col(2 * n_pairs),
                  pl.BlockSpec((None,) + bias.shape[1:], lambda b, p: (p, 0, 0, 0, 0))],
        out_specs=pl.BlockSpec((None, seq, LANES), lambda b, p: (b, 0, p)),
        out_shape=jax.ShapeDtypeStruct((bsz, seq, ATT_WIDTH), F32),
        scratch_shapes=[pltpu.VMEM(slabs, F32), pltpu.VMEM(slabs, F32), pltpu.VMEM(slabs, F32),
                        pltpu.VMEM((2, ATT_UNROLL, HEADS_PER_TILE * Q_BLOCK, K_BLOCK), F32)],
        compiler_params=_cparams(("parallel", "parallel")),
    )(qkv, qkv, qkv, bias)


POOL_HALO = max(POOL_WINDOWS) // 2


POOL_MARGIN = 2 * POOL_HALO


def _pool_centred(t, n_tiles, u_ref, prev_ref, next_ref, inv_cnt_ref, pad_ref):
    assert POOL_WINDOWS == tuple(2 << k for k in range(len(POOL_WINDOWS)))
    ts = u_ref.shape[0]
    m = POOL_MARGIN
    n_ext = ts + 2 * POOL_HALO
    edge = jnp.zeros((POOL_HALO, POOL_WIDTH), F32)
    for k in range(len(POOL_WINDOWS) - 1):
        pad_ref[k, pl.ds(0, POOL_HALO), :] = edge
        pad_ref[k, pl.ds(m + ts + POOL_HALO, POOL_HALO), :] = edge
    pad_ref[0, pl.ds(m - POOL_HALO, POOL_HALO), :] = jnp.where(t > 0, prev_ref[...], 0.0)
    pad_ref[0, pl.ds(m, ts), :] = u_ref[...]
    pad_ref[0, pl.ds(m + ts, POOL_HALO), :] = jnp.where(t < n_tiles - 1, next_ref[...], 0.0)

    sums = []
    for k, win in enumerate(POOL_WINDOWS):
        lo, hi = (1, 0) if k == 0 else (win // 4, -(win // 4))
        if k < len(POOL_WINDOWS) - 1:
            pad_ref[k + 1, pl.ds(POOL_HALO, n_ext), :] = (pad_ref[k, pl.ds(POOL_HALO - lo, n_ext), :]
                                                         + pad_ref[k, pl.ds(POOL_HALO - hi, n_ext), :])
            sums.append(pad_ref[k + 1, pl.ds(m, ts), :])
        else:
            sums.append(pad_ref[k, pl.ds(m - lo, ts), :] + pad_ref[k, pl.ds(m - hi, ts), :])

    lane_win = lax.broadcasted_iota(jnp.int32, (1, POOL_WIDTH), 1) // HEAD_DIM
    total = sums[0]
    for gi in range(1, len(POOL_WINDOWS)):
        total = jnp.where(lane_win == gi, sums[gi], total)
    return total * inv_cnt_ref[...] - pad_ref[0, pl.ds(m, ts), :]


def _pool_inverse_counts(seq):
    pos = jnp.arange(seq, dtype=jnp.int32)[:, None]
    win = jnp.repeat(jnp.asarray(POOL_WINDOWS, jnp.int32), HEAD_DIM)[None, :]
    cnt = jnp.minimum(pos + win - win // 2, seq) - jnp.maximum(pos - win // 2, 0)
    return 1.0 / cnt.astype(F32)


def _pool_blockdiag(w_pool):
    n_g = w_pool.shape[0]
    eye = jnp.eye(n_g, dtype=w_pool.dtype)
    return jnp.einsum('gcd,gh->gchd', w_pool, eye).reshape(n_g * HEAD_DIM, n_g * HEAD_DIM)


def _hgrn_constants():
    c = HGRN_CHUNK
    t = np.arange(c)
    s_col = np.tile(t, HEADS_PER_TILE)
    cum, masks = [], []
    for direction in range(2):
        tri = (t[None, :] <= t[:, None]) if direction == 0 else (t[None, :] >= t[:, None])
        tri = tri.astype(np.float32)
        mats, qms, bms = [tri], [], []
        for level in range(HGRN_LEVELS):
            half = HGRN_SUB << level
            start = (t // (2 * half)) * (2 * half)
            boundary = start + (half - 1 if direction == 0 else half)
            mats.append(tri[boundary])
            q_right = (t // half) % 2 == 1
            qm = q_right if direction == 0 else ~q_right
            qms.append(np.broadcast_to(qm[:, None], (c, LANES)))
            bms.append(t[:, None] // (2 * half) == s_col[None, :] // (2 * half))
        mats.append(tri[(t // HGRN_SUB) * HGRN_SUB + (0 if direction == 0 else HGRN_SUB - 1)])
        same_sub = t[:, None] // HGRN_SUB == s_col[None, :] // HGRN_SUB
        causal = (s_col[None, :] <= t[:, None]) if direction == 0 else (s_col[None, :] >= t[:, None])
        cum.append(np.tile(np.concatenate(mats, axis=0), (1, 3)))
        masks.append(np.stack(qms + bms[:HGRN_LEVELS - 1] + [same_sub & causal]).astype(np.float32))
    lane_head = np.arange(LANES) // HEAD_DIM
    sel = np.concatenate([(lane_head[:, None] == lane_head[None, :]) & (s_col[None, :] % HGRN_SUB == j)
                          for j in range(HGRN_SUB)], axis=0).astype(np.float32)
    return (jnp.asarray(np.stack(cum), BF16), jnp.asarray(np.stack(masks), F32), jnp.asarray(sel, BF16))


def _split3(x):
    x1 = x.astype(BF16)
    r1 = x - x1.astype(F32)
    x2 = r1.astype(BF16)
    x3 = (r1 - x2.astype(F32)).astype(BF16)
    return x1, x2, x3


def _by_head(x):
    lo = lax.broadcasted_iota(jnp.int32, x.shape, 1) < HEAD_DIM
    return jnp.concatenate([jnp.where(lo, x, 0.0), jnp.where(lo, 0.0, x)], axis=0).astype(BF16)


def _hgrn_gates(st, slot, cum_ref, ball_scr, k_scr):
    fl = st["f_ref"][st["at"]]
    lb = st["lb_ref"][:, st["cols"]]
    e = jnp.exp(-jnp.abs(fl))
    one_plus_e = 1.0 + e
    log_sig = jnp.minimum(fl, 0.0) - jnp.log(one_plus_e)
    rcp = 1.0 / one_plus_e
    k_scr[slot, st["stream"]] = (1.0 - lb) * jnp.where(fl >= 0, e * rcp, rcp)
    log_lb = jnp.log(lb)
    cc = jnp.log(1.0 - lb) + log_sig
    g = jnp.maximum(log_lb, cc) + jnp.log(1.0 + jnp.exp(-jnp.abs(log_lb - cc)))
    cum = cum_ref[st["direction"]]
    ball_scr[slot, st["stream"]] = jnp.dot(cum, jnp.concatenate(_split3(g * LOG2_E), axis=0),
                                           preferred_element_type=F32)


def _hgrn_products(st, slot, mask_ref, sel_ref, state_ref, ball_scr, k_scr, out_scr, *, pairwise):
    c = HGRN_CHUNK
    direction, stream = st["direction"], st["stream"]
    q, v = st["q_ref"][st["at"]], st["v_ref"][st["at"]]
    kk = k_scr[slot, stream]
    ball = ball_scr[slot, stream]
    b = ball[0:c]
    b_edge = b[c - 1:c] if direction == 0 else b[0:1]

    state = state_ref[stream]
    out_scr[slot, stream, 0] = _nt((q * jnp.exp2(b)).astype(BF16), state.astype(BF16))
    k_out = kk * jnp.exp2(b_edge - b)
    upd = _tn(v.astype(BF16), k_out.astype(BF16))
    r128 = lax.broadcasted_iota(jnp.int32, (LANES, LANES), 0) // HEAD_DIM
    c128 = lax.broadcasted_iota(jnp.int32, (LANES, LANES), 1) // HEAD_DIM
    state_ref[stream] = jnp.where(r128 == c128, state * jnp.exp2(b_edge) + upd, 0.0)

    att = None
    for level in range(HGRN_LEVELS):
        beta = ball[(level + 1) * c:(level + 2) * c]
        qm = mask_ref[direction, level]
        decay = jnp.exp2(-jnp.abs(b - beta))
        ql = q * decay * qm
        kl = kk * decay * (1.0 - qm)
        a = _nt(ql.astype(BF16), _by_head(kl))
        if level < HGRN_LEVELS - 1:
            a = a * mask_ref[direction, HGRN_LEVELS + level]
        att = a if att is None else att + a

    if pairwise:
        pair_cols = []
        for j in range(HGRN_SUB):
            pieces = []
            for u in range(c // HGRN_SUB):
                r = u * HGRN_SUB
                b_row = ball_scr[slot, stream, pl.ds(r + j, 1), :]
                k_row = k_scr[slot, stream, pl.ds(r + j, 1), :]
                pieces.append((q[r:r + HGRN_SUB] * k_row)
                              * jnp.exp2(jnp.minimum(b[r:r + HGRN_SUB] - b_row, 0.0)))
            pair_cols.append(jnp.concatenate(pieces, axis=0).astype(BF16))
        diag = jnp.dot(jnp.concatenate(pair_cols, axis=1), sel_ref[...], preferred_element_type=F32)
    else:
        ref = ball[(HGRN_LEVELS + 1) * c:(HGRN_LEVELS + 2) * c]
        diag = _nt((q * jnp.exp2(b - ref)).astype(BF16), _by_head(kk * jnp.exp2(ref - b)))
    out_scr[slot, stream, 1] = att + diag * mask_ref[direction, 2 * HGRN_LEVELS - 1]


def _hgrn_finish(st, slot, out_scr):
    v = st["v_ref"][st["at"]]
    st["o_ref"][st["at"]] = out_scr[slot, st["stream"], 0] + jnp.dot(
        out_scr[slot, st["stream"], 1].astype(BF16), _by_head(v), preferred_element_type=F32)


def _hgrn_body(qf_ref, vf_ref, ff_ref, qb_ref, vb_ref, fb_ref, lbf_ref, lbb_ref,
               cum_ref, mask_ref, sel_ref, of_ref, ob_ref, state_ref, ball_scr, k_scr, out_scr):
    @pl.when(pl.program_id(1) == 0)
    def _():
        state_ref[...] = jnp.zeros_like(state_ref)

    n_batch, n_rows = qf_ref.shape[0], qf_ref.shape[1]
    n_chunks = n_rows // HGRN_CHUNK
    n_tiles = HGRN_WIDTH // LANES
    per_direction = ((qf_ref, vf_ref, ff_ref, lbf_ref, of_ref), (qb_ref, vb_ref, fb_ref, lbb_ref, ob_ref))

    def streams_of(i):
        streams = []
        for seq_i in range(n_batch):
            for direction, (q_ref, v_ref, f_ref, lb_ref, o_ref) in enumerate(per_direction):
                chunk = i if direction == 0 else n_chunks - 1 - i
                rows = pl.ds(pl.multiple_of(chunk * HGRN_CHUNK, HGRN_CHUNK), HGRN_CHUNK)
                for tile in range(n_tiles):
                    cols = slice(tile * LANES, (tile + 1) * LANES)
                    streams.append(dict(direction=direction, cols=cols, at=(seq_i, rows, cols),
                                        stream=(seq_i * 2 + direction) * n_tiles + tile, q_ref=q_ref,
                                        v_ref=v_ref, f_ref=f_ref, lb_ref=lb_ref, o_ref=o_ref))
        return streams

    for st in streams_of(0):
        _hgrn_gates(st, 0, cum_ref, ball_scr, k_scr)
    out_scr[1] = jnp.zeros(out_scr.shape[1:], out_scr.dtype)

    def step_group(group, carry, *, pairwise):
        for k in range(HGRN_UNROLL):
            slot = k % 2
            i = HGRN_UNROLL * group + k
            for st in streams_of(jnp.maximum(i - 1, 0)):
                _hgrn_finish(st, 1 - slot, out_scr)
            for st in streams_of(jnp.minimum(i + 1, n_chunks - 1)):
                _hgrn_gates(st, 1 - slot, cum_ref, ball_scr, k_scr)
            for st in streams_of(i):
                _hgrn_products(st, slot, mask_ref, sel_ref, state_ref, ball_scr, k_scr, out_scr,
                               pairwise=pairwise)
        return carry

    factorisable = jnp.maximum(jnp.max(-ff_ref[...]), jnp.max(-fb_ref[...])) <= HGRN_SAFE_GATE
    for pairwise in (False, True):
        @pl.when(factorisable != pairwise)
        def _(pairwise=pairwise):
            lax.fori_loop(0, n_chunks // HGRN_UNROLL, functools.partial(step_group, pairwise=pairwise), 0)

    for st in streams_of(n_chunks - 1):
        _hgrn_finish(st, (n_chunks - 1) % 2, out_scr)


def hgrn_scan(rest, lb_fwd, lb_bwd, *, ts=1024):
    bsz, seq, _ = rest.shape
    ts = min(ts, seq)
    nt = seq // ts
    n_batch = HGRN_BATCH if bsz % HGRN_BATCH == 0 else 1
    n_streams = n_batch * 2 * HGRN_WIDTH // LANES
    cum, masks, sel = _hgrn_constants()

    def fwd(c0):
        return pl.BlockSpec((n_batch, ts, HGRN_WIDTH), lambda b, j: (b, j, c0))

    def bwd(c0):
        return pl.BlockSpec((n_batch, ts, HGRN_WIDTH), lambda b, j: (b, nt - 1 - j, c0))

    return pl.pallas_call(
        _hgrn_body,
        grid=(bsz // n_batch, nt),
        in_specs=[fwd(1), fwd(2), fwd(3), bwd(1), bwd(2), bwd(4),
                  _resident((1, HGRN_WIDTH)), _resident((1, HGRN_WIDTH)),
                  _resident(cum.shape), _resident(masks.shape), _resident(sel.shape)],
        out_specs=[fwd(0), bwd(0)],
        out_shape=[jax.ShapeDtypeStruct((bsz, seq, HGRN_WIDTH), F32)] * 2,
        scratch_shapes=[pltpu.VMEM((n_streams, LANES, LANES), F32),
                        pltpu.VMEM((2, n_streams) + cum.shape[1:2] + (LANES,), F32),
                        pltpu.VMEM((2, n_streams, HGRN_CHUNK, LANES), F32),
                        pltpu.VMEM((2, n_streams, 2, HGRN_CHUNK, LANES), F32)],
        compiler_params=_cparams(("parallel", "arbitrary")),
    )(rest, rest, rest, rest, rest, rest, lb_fwd.reshape(1, HGRN_WIDTH), lb_bwd.reshape(1, HGRN_WIDTH),
      cum, masks, sel)


def _post_body(x_ref, att_ref, u_ref, prev_ref, next_ref, inv_cnt_ref, of_ref, ob_ref, gate_ref,
               pw_ref, ps_ref, gain_ref, bd_ref, w_ref, g2_ref, wg_ref, wu_ref, wd_ref,
               o_ref, pad_ref, a_ref):
    tile, n_tiles = pl.program_id(1), pl.num_programs(1)
    centred = _pool_centred(tile, n_tiles, u_ref, prev_ref, next_ref, inv_cnt_ref, pad_ref)
    pooled = jnp.dot(centred.astype(BF16), pw_ref[...], preferred_element_type=F32) * ps_ref[...]
    o = of_ref[...] + ob_ref[...]
    y = o * lax.rsqrt(_head_mean_sq(o, bd_ref[...]) + EPS) * gain_ref[...]
    gate = gate_ref[...]
    rec = (y * (gate * jax.nn.sigmoid(gate))).astype(BF16)
    mixed = jnp.concatenate([att_ref[...].astype(BF16), pooled.astype(BF16), rec], axis=1)
    x = x_ref[...] + jnp.dot(mixed, w_ref[...], preferred_element_type=F32)
    o_ref[...] = _half_ffn(x, g2_ref, wg_ref, wu_ref, wd_ref, a_ref)


def post_mixer(x3, y_att, rest, o_fwd, o_bwd, pool_w, pool_scale, out_gain, w_out,
               ffn_gain, wg, wu, wd, *, tm=512):
    bsz, seq, d = x3.shape
    dff = wg.shape[1]
    tm = min(tm, seq)
    gate_tile = (rest.shape[2] - HGRN_WIDTH) // HGRN_WIDTH
    per_tile = tm // POOL_HALO
    last_halo = seq // POOL_HALO - 1

    def rowblock(width, col=0):
        return pl.BlockSpec((None, tm, width), lambda b, t: (b, t, col))

    return pl.pallas_call(
        _post_body,
        grid=(bsz, seq // tm),
        in_specs=[rowblock(d), rowblock(ATT_WIDTH), rowblock(POOL_WIDTH),
                  pl.BlockSpec((None, POOL_HALO, POOL_WIDTH),
                               lambda b, t: (b, jnp.maximum(t * per_tile - 1, 0), 0)),
                  pl.BlockSpec((None, POOL_HALO, POOL_WIDTH),
                               lambda b, t: (b, jnp.minimum((t + 1) * per_tile, last_halo), 0)),
                  pl.BlockSpec((tm, POOL_WIDTH), lambda b, t: (t, 0)),
                  rowblock(HGRN_WIDTH), rowblock(HGRN_WIDTH), rowblock(HGRN_WIDTH, gate_tile),
                  _resident((POOL_WIDTH, POOL_WIDTH)), _resident((1, POOL_WIDTH)),
                  _resident((1, HGRN_WIDTH)), _resident((HGRN_WIDTH, HGRN_WIDTH)), _resident(w_out.shape),
                  _resident((1, d)), _resident((d, dff)), _resident((d, dff)), _resident((dff, d))],
        out_specs=rowblock(d),
        out_shape=jax.ShapeDtypeStruct((bsz, seq, d), F32),
        scratch_shapes=[pltpu.VMEM((len(POOL_WINDOWS), tm + 2 * POOL_MARGIN, POOL_WIDTH), F32),
                        pltpu.VMEM((tm, dff), BF16)],
        compiler_params=_cparams(("parallel", "parallel")),
    )(x3, y_att, rest, rest, rest, _pool_inverse_counts(seq), o_fwd, o_bwd, rest,
      _pool_blockdiag(pool_w).astype(BF16), pool_scale.reshape(1, POOL_WIDTH),
      jnp.tile(out_gain, HGRN_WIDTH // HEAD_DIM).reshape(1, HGRN_WIDTH), _head_blockdiag(HGRN_WIDTH),
      w_out, ffn_gain.reshape(1, d), wg, wu, wd)


def kernel(x, ffn1_norm, ffn1_w_gate, ffn1_w_up, ffn1_w_down, mix_norm, w_in, q_norm, k_norm, rel_bias,
           pool_w, pool_scale, hgrn_lb_logits, hgrn_norm, w_out, ffn2_norm, ffn2_w_gate, ffn2_w_up,
           ffn2_w_down):
    bsz, seq, d = x.shape
    depth = w_in.shape[0]
    n = bsz * seq
    h = x.astype(F32).reshape(n, d)
    bias = attention_bias(rel_bias)
    lb_cum = jnp.cumsum(jax.nn.softmax(hgrn_lb_logits.astype(F32), axis=1), axis=1)
    lb_all = lb_cum - lb_cum[:, :1]
    for l in range(depth):
        h, qkv, rest = pre_mixer(h, seq, ffn1_norm[l], ffn1_w_gate[l].astype(BF16),
                                 ffn1_w_up[l].astype(BF16), ffn1_w_down[l].astype(BF16),
                                 mix_norm[l], w_in[l].astype(BF16), q_norm[l], k_norm[l])
        rest = rest.reshape(bsz, seq, -1)
        y_att = attention(qkv, bias)
        o_fwd, o_bwd = hgrn_scan(rest, lb_all[0, l], lb_all[1, l])
        h = post_mixer(h.reshape(bsz, seq, d), y_att, rest, o_fwd, o_bwd, pool_w[l], pool_scale[l],
                       hgrn_norm[l], w_out[l].astype(BF16), ffn2_norm[l], ffn2_w_gate[l].astype(BF16),
                       ffn2_w_up[l].astype(BF16), ffn2_w_down[l].astype(BF16)).reshape(n, d)
    return h.reshape(bsz, seq, d).astype(x.dtype)
```

```python
import functools
import math

import numpy as np
import jax
import jax.numpy as jnp
from jax import lax
from jax.experimental import pallas as pl
from jax.experimental.pallas import tpu as pltpu

F32 = jnp.float32
BF16 = jnp.bfloat16

LANES = 128
SUBLANES = 8
MXU_COLS = 256
VMEM_BYTES_V7X = 64 * 1024 * 1024
VMEM_LIMIT = VMEM_BYTES_V7X - 8 * 1024 * 1024

HEAD_DIM = 64
HEADS_PER_TILE = LANES // HEAD_DIM
ATT_WIDTH = 512
POOL_WIDTH = 256
HGRN_WIDTH = 256
POOL_WINDOWS = (2, 4, 8, 16)
DILATIONS = (1, 4, 16)
ATT_SIDE = 64
NUM_BUCKETS = 32
MAX_DISTANCE = 1024
EPS = 1e-6
NEG = -1e30
LOG2_E = math.log2(math.e)

Q_BLOCK = 128
K_BLOCK = Q_BLOCK + 2 * ATT_SIDE
ATT_UNROLL = 4
ATT_SLABS = 4
HGRN_CHUNK = 64
HGRN_SUB = SUBLANES
HGRN_LEVELS = 3
HGRN_UNROLL = 4
HGRN_BATCH = 2
HGRN_SAFE_GATE = 8.0


def _cparams(sem):
    return pltpu.CompilerParams(dimension_semantics=sem, vmem_limit_bytes=VMEM_LIMIT)


def _resident(shape):
    nd = len(shape)
    return pl.BlockSpec(shape, lambda *_: (0,) * nd, pipeline_mode=pl.Buffered(1))


def _nt(a, b):
    return lax.dot_general(a, b, (((1,), (1,)), ((), ())), preferred_element_type=F32)


def _tn(a, b):
    return lax.dot_general(a, b, (((0,), (0,)), ((), ())), preferred_element_type=F32)


def _head_blockdiag(width=LANES):
    r = np.arange(width) // HEAD_DIM
    return jnp.asarray(r[:, None] == r[None, :], dtype=BF16)


def _head_mean_sq(v, bd):
    return jnp.dot((v * v).astype(BF16), bd, preferred_element_type=F32) * (1.0 / HEAD_DIM)


def _rmsnorm_bf16(x, gain):
    return (x * lax.rsqrt(jnp.mean(x * x, axis=-1, keepdims=True) + EPS) * gain).astype(BF16)


def _half_ffn(x, g_ref, wg_ref, wu_ref, wd_ref, a_ref, between=()):
    h = _rmsnorm_bf16(x, g_ref[...])
    between = list(between)
    for c in range(wg_ref.shape[1] // MXU_COLS):
        sl = slice(c * MXU_COLS, (c + 1) * MXU_COLS)
        gate = jnp.dot(h, wg_ref[:, sl], preferred_element_type=F32)
        up = jnp.dot(h, wu_ref[:, sl], preferred_element_type=F32)
        a_ref[:, sl] = (gate * jax.nn.sigmoid(gate) * up).astype(BF16)
        thunk = between.pop(0) if between else None
        if thunk is not None:
            thunk()
    for thunk in between:
        thunk()
    return x + 0.5 * jnp.dot(a_ref[...], wd_ref[...], preferred_element_type=F32)


def _pre_body(x_ref, g1_ref, wg_ref, wu_ref, wd_ref, g_ref, w_ref, qg_ref, kg_ref, bd_ref,
              h_ref, qkv_ref, rest_ref, a_ref, perm_scr, hn_scr):
    @pl.when(pl.program_id(0) == 0)
    def _():
        hn_scr[...] = jnp.zeros_like(hn_scr)

    bd = bd_ref[...]
    wc = MXU_COLS
    slab_rows = qkv_ref.shape[1]
    n_att = 3 * ATT_WIDTH

    def to_slabs(c, z):
        for t in range(wc // LANES):
            perm_scr[c, t] = z[:, t * LANES:(t + 1) * LANES]
            for r in range(ATT_SLABS):
                lanes = slice(c * wc + t * LANES, c * wc + (t + 1) * LANES)
                qkv_ref[r, :, lanes] = perm_scr[c, t, pl.ds(r, slab_rows, stride=ATT_SLABS), :]

    projected = {}

    hn = hn_scr[...]

    def project(c):
        projected[c] = jnp.dot(hn, w_ref[:, c * wc:(c + 1) * wc], preferred_element_type=F32)

    def finish(c):
        sl = slice(c * wc, (c + 1) * wc)
        z = projected.pop(c)
        if c >= n_att // wc:
            rest_ref[:, c * wc - n_att:(c + 1) * wc - n_att] = z
            return
        if c < 2 * ATT_WIDTH // wc:
            if c < ATT_WIDTH // wc:
                gain = qg_ref[:, sl] * (HEAD_DIM ** -0.5 * LOG2_E)
            else:
                gain = kg_ref[:, c * wc - ATT_WIDTH:(c + 1) * wc - ATT_WIDTH]
            z = z * lax.rsqrt(_head_mean_sq(z, bd) + EPS) * gain
        to_slabs(c, z)

    def skewed(c):
        def thunk():
            if c > 0:
                finish(c - 1)
            if c < n_chunks:
                project(c)
        return thunk

    n_chunks = w_ref.shape[1] // wc
    thunks = [skewed(c) for c in range(n_chunks + 1)]
    lead, trail = 2, 3
    for thunk in thunks[:lead]:
        thunk()
    x = _half_ffn(x_ref[...], g1_ref, wg_ref, wu_ref, wd_ref, a_ref, between=thunks[lead:-trail])
    for thunk in thunks[-trail:]:
        thunk()
    h_ref[...] = x
    hn_scr[...] = _rmsnorm_bf16(x, g_ref[...])


def pre_mixer(x2, seq, ffn_gain, wg, wu, wd, gain, w_in, q_gain, k_gain, *, tm=512):
    n, d = x2.shape
    dff = wg.shape[1]
    cols = w_in.shape[1]
    n_att = 3 * ATT_WIDTH
    tm = min(tm, seq)
    tiles = seq // tm
    last = n // tm - 1

    def ffn_tile(i):
        return jnp.minimum(i, last)

    def proj_tile(i):
        return jnp.maximum(i - 1, 0)

    return pl.pallas_call(
        _pre_body,
        grid=(n // tm + 1,),
        in_specs=[pl.BlockSpec((tm, d), lambda i: (ffn_tile(i), 0)),
                  _resident((1, d)), _resident((d, dff)), _resident((d, dff)), _resident((dff, d)),
                  _resident((1, d)), _resident((d, cols)),
                  _resident((1, ATT_WIDTH)), _resident((1, ATT_WIDTH)), _resident((MXU_COLS, MXU_COLS))],
        out_specs=[pl.BlockSpec((tm, d), lambda i: (ffn_tile(i), 0)),
                   pl.BlockSpec((None, ATT_SLABS, tm // ATT_SLABS, n_att),
                                lambda i: (proj_tile(i) // tiles, 0, proj_tile(i) % tiles, 0)),
                   pl.BlockSpec((tm, cols - n_att), lambda i: (proj_tile(i), 0))],
        out_shape=[jax.ShapeDtypeStruct((n, d), F32),
                   jax.ShapeDtypeStruct((n // seq, ATT_SLABS, seq // ATT_SLABS, n_att), F32),
                   jax.ShapeDtypeStruct((n, cols - n_att), F32)],
        scratch_shapes=[pltpu.VMEM((tm, dff), BF16),
                        pltpu.VMEM((n_att // MXU_COLS, MXU_COLS // LANES, tm, LANES), F32),
                        pltpu.VMEM((tm, d), BF16)],
        compiler_params=_cparams(("arbitrary",)),
    )(x2, ffn_gain.reshape(1, d), wg, wu, wd, gain.reshape(1, d), w_in,
      q_gain.reshape(1, ATT_WIDTH), k_gain.reshape(1, ATT_WIDTH), _head_blockdiag(MXU_COLS))


def _t5_bucket(rel):
    half = NUM_BUCKETS // 2
    max_exact = half // 2
    base = jnp.where(rel > 0, half, 0)
    n = jnp.abs(rel)
    nf = jnp.maximum(n, 1).astype(F32)
    large = max_exact + (jnp.log(nf / max_exact) / math.log(MAX_DISTANCE / max_exact)
                         * (half - max_exact)).astype(jnp.int32)
    large = jnp.minimum(large, half - 1)
    return base + jnp.where(n < max_exact, n, large)


def _bias_buckets():
    out = []
    for dil in DILATIONS:
        qi, ki = np.arange(Q_BLOCK), np.arange(K_BLOCK)
        if dil == 1:
            qi = qi.reshape(-1, ATT_SLABS).T.reshape(-1)
            ki = ki.reshape(-1, ATT_SLABS).T.reshape(-1)
        qi, ki = jnp.asarray(qi)[:, None], jnp.asarray(ki)[None, :]
        per_var = []
        for var in range(3):
            rel = ki - var * ATT_SIDE - qi
            per_var.append(jnp.where(jnp.abs(rel) <= ATT_SIDE, _t5_bucket(rel * dil), -1))
        out.append(jnp.stack(per_var))
    return jnp.stack(out).astype(jnp.int32)


def _bias_body(tbl_ref, bk_ref, o_ref):
    hp = pl.program_id(0)
    for var in range(3):
        bk = bk_ref[var]
        for h in range(HEADS_PER_TILE):
            head = hp * HEADS_PER_TILE + h
            acc = jnp.full(bk.shape, NEG, F32)
            for b in range(NUM_BUCKETS):
                acc = jnp.where(bk == b, tbl_ref[b, head] * LOG2_E, acc)
            o_ref[var, h] = acc


def attention_bias(rel_bias):
    n_pairs = rel_bias.shape[1] // HEADS_PER_TILE
    n_g = len(DILATIONS)
    return pl.pallas_call(
        _bias_body,
        grid=(n_pairs, n_g),
        in_specs=[pl.BlockSpec(memory_space=pltpu.SMEM),
                  pl.BlockSpec((None, 3, Q_BLOCK, K_BLOCK), lambda p, g: (g, 0, 0, 0))],
        out_specs=pl.BlockSpec((None, None, 3, HEADS_PER_TILE, Q_BLOCK, K_BLOCK),
                               lambda p, g: (p, g, 0, 0, 0, 0)),
        out_shape=jax.ShapeDtypeStruct((n_pairs, n_g, 3, HEADS_PER_TILE, Q_BLOCK, K_BLOCK), F32),
        compiler_params=_cparams(("parallel", "parallel")),
    )(rel_bias.astype(F32), _bias_buckets())


def _attn_body(q_ref, k_ref, v_ref, bias_ref, o_ref, acc_ref, m_ref, l_ref, s_scr):
    slab_len = q_ref.shape[1]
    seq = ATT_SLABS * slab_len
    lo = lax.broadcasted_iota(jnp.int32, (Q_BLOCK, LANES), 1) < HEAD_DIM
    ones = jnp.ones((K_BLOCK, LANES), BF16)

    def window(n, sub_len):
        qs = n * Q_BLOCK
        ks = jnp.clip(qs - ATT_SIDE, 0, sub_len - K_BLOCK)
        var = jnp.where(n > 0, 1, 0) + jnp.where(n == sub_len // Q_BLOCK - 1, 1, 0)
        return qs, ks, var

    def pieces(dil, it, u):
        if dil == 1:
            qs, ks, var = window(it * ATT_UNROLL + u, seq)
            q0 = pl.multiple_of(qs // ATT_SLABS, Q_BLOCK // ATT_SLABS)
            k0 = pl.multiple_of(ks // ATT_SLABS, ATT_SIDE // ATT_SLABS)
            return ([(r, pl.ds(q0, Q_BLOCK // ATT_SLABS)) for r in range(ATT_SLABS)],
                    [(r, pl.ds(k0, K_BLOCK // ATT_SLABS)) for r in range(ATT_SLABS)], var)
        if dil == ATT_SLABS:
            qs, ks, var = window(it, slab_len)
            return ([(u, pl.ds(pl.multiple_of(qs, Q_BLOCK), Q_BLOCK))],
                    [(u, pl.ds(pl.multiple_of(ks, ATT_SIDE), K_BLOCK))], var)
        step = dil // ATT_SLABS
        a, n = it % step, it // step
        qs, ks, var = window(n, slab_len // step)
        return ([(u, pl.ds(a + step * qs, Q_BLOCK, stride=step))],
                [(u, pl.ds(a + step * ks, K_BLOCK, stride=step))], var)

    def load(ref, idx):
        parts = [ref[r, rows, :] for r, rows in idx]
        return parts[0] if len(parts) == 1 else jnp.concatenate(parts, axis=0)

    def store(ref, idx, val):
        n = val.shape[0] // len(idx)
        for j, (r, rows) in enumerate(idx):
            ref[r, rows, :] = val[j * n:(j + 1) * n]

    def scores(g, q_idx, k_idx, var):
        q = load(q_ref, q_idx)
        k = load(k_ref, k_idx).astype(BF16)
        qq = jnp.concatenate([jnp.where(lo, q, 0.0), jnp.where(lo, 0.0, q)], axis=0).astype(BF16)
        return _nt(qq, k) + bias_ref[g, var]

    def weighted(k_idx, s):
        vo = jnp.concatenate([load(v_ref, k_idx).astype(BF16), ones], axis=1)
        m2 = jnp.max(s, axis=1, keepdims=True)
        ol = jnp.dot(jnp.exp2(s - m2).astype(BF16), vo, preferred_element_type=F32)
        top, bot = ol[:Q_BLOCK], ol[Q_BLOCK:]
        m = jnp.where(lo, jnp.broadcast_to(m2[:Q_BLOCK], (Q_BLOCK, LANES)),
                      jnp.broadcast_to(m2[Q_BLOCK:], (Q_BLOCK, LANES)))
        return (jnp.where(lo, top[:, :LANES], bot[:, :LANES]),
                jnp.where(lo, top[:, LANES:], bot[:, LANES:]), m)

    def merge(g, q_idx, o, l, m):
        if g > 0:
            m_old = load(m_ref, q_idx)
            m_new = jnp.maximum(m_old, m)
            a_old, a_cur = jnp.exp2(m_old - m_new), jnp.exp2(m - m_new)
            o = load(acc_ref, q_idx) * a_old + o * a_cur
            l = load(l_ref, q_idx) * a_old + l * a_cur
            m = m_new
        store(acc_ref, q_idx, o)
        store(l_ref, q_idx, l)
        store(m_ref, q_idx, m)

    n_iter = seq // Q_BLOCK // ATT_UNROLL
    for g, dil in enumerate(DILATIONS):
        def score_into(slot, it, g=g, dil=dil):
            for u in range(ATT_UNROLL):
                q_idx, k_idx, var = pieces(dil, it, u)
                s_scr[slot, u] = scores(g, q_idx, k_idx, var)

        def step_pair(pair, carry, g=g, dil=dil, score_into=score_into):
            for slot in range(2):
                it = 2 * pair + slot
                score_into(1 - slot, jnp.minimum(it + 1, n_iter - 1))
                blocks = [pieces(dil, it, u) for u in range(ATT_UNROLL)]
                done = [weighted(k_idx, s_scr[slot, u]) for u, (_, k_idx, _) in enumerate(blocks)]
                for (q_idx, _, _), res in zip(blocks, done):
                    merge(g, q_idx, *res)
            return carry

        score_into(0, 0)
        lax.fori_loop(0, n_iter // 2, step_pair, 0)

    def normalise(i, c):
        start = pl.multiple_of(i * Q_BLOCK, Q_BLOCK)
        for r in range(ATT_SLABS):
            rows = pl.ds(start, Q_BLOCK)
            o_ref[pl.ds(r + ATT_SLABS * start, Q_BLOCK, stride=ATT_SLABS), :] = (
                acc_ref[r, rows, :] / l_ref[r, rows, :])
        return c

    lax.fori_loop(0, slab_len // Q_BLOCK, normalise, 0)


def attention(qkv, bias):
    bsz, n_slab, slab_len, _ = qkv.shape
    seq = n_slab * slab_len
    n_pairs = ATT_WIDTH // LANES
    assert n_slab == ATT_SLABS == ATT_UNROLL == DILATIONS[1] and DILATIONS[2] % ATT_SLABS == 0
    assert seq % (max(DILATIONS) * Q_BLOCK) == 0 and seq // max(DILATIONS) >= K_BLOCK
    bias = bias.reshape(bias.shape[:3] + (HEADS_PER_TILE * Q_BLOCK, K_BLOCK))
    slabs = (ATT_SLABS, slab_len, LANES)

    def col(c0):
        return pl.BlockSpec((None,) + slabs, lambda b, p: (b, 0, 0, c0 + p))

    return pl.pallas_call(
        _attn_body,
        grid=(bsz, n_pairs),
        in_specs=[col(0), col(n_pairs), col(2 * n_pairs),
                  pl.BlockSpec((None,) + bias.shape[1:], lambda b, p: (p, 0, 0, 0, 0))],
        out_specs=pl.BlockSpec((None, seq, LANES), lambda b, p: (b, 0, p)),
        out_shape=jax.ShapeDtypeStruct((bsz, seq, ATT_WIDTH), F32),
        scratch_shapes=[pltpu.VMEM(slabs, F32), pltpu.VMEM(slabs, F32), pltpu.VMEM(slabs, F32),
                        pltpu.VMEM((2, ATT_UNROLL, HEADS_PER_TILE * Q_BLOCK, K_BLOCK), F32)],
        compiler_params=_cparams(("parallel", "parallel")),
    )(qkv, qkv, qkv, bias)


POOL_HALO = max(POOL_WINDOWS) // 2


POOL_MARGIN = 2 * POOL_HALO


def _pool_centred(t, n_tiles, u_ref, prev_ref, next_ref, inv_cnt_ref, pad_ref):
    assert POOL_WINDOWS == tuple(2 << k for k in range(len(POOL_WINDOWS)))
    ts = u_ref.shape[0]
    m = POOL_MARGIN
    n_ext = ts + 2 * POOL_HALO
    edge = jnp.zeros((POOL_HALO, POOL_WIDTH), F32)
    for k in range(len(POOL_WINDOWS) - 1):
        pad_ref[k, pl.ds(0, POOL_HALO), :] = edge
        pad_ref[k, pl.ds(m + ts + POOL_HALO, POOL_HALO), :] = edge
    pad_ref[0, pl.ds(m - POOL_HALO, POOL_HALO), :] = jnp.where(t > 0, prev_ref[...], 0.0)
    pad_ref[0, pl.ds(m, ts), :] = u_ref[...]
    pad_ref[0, pl.ds(m + ts, POOL_HALO), :] = jnp.where(t < n_tiles - 1, next_ref[...], 0.0)

    sums = []
    for k, win in enumerate(POOL_WINDOWS):
        lo, hi = (1, 0) if k == 0 else (win // 4, -(win // 4))
        if k < len(POOL_WINDOWS) - 1:
            pad_ref[k + 1, pl.ds(POOL_HALO, n_ext), :] = (pad_ref[k, pl.ds(POOL_HALO - lo, n_ext), :]
                                                         + pad_ref[k, pl.ds(POOL_HALO - hi, n_ext), :])
            sums.append(pad_ref[k + 1, pl.ds(m, ts), :])
        else:
            sums.append(pad_ref[k, pl.ds(m - lo, ts), :] + pad_ref[k, pl.ds(m - hi, ts), :])

    lane_win = lax.broadcasted_iota(jnp.int32, (1, POOL_WIDTH), 1) // HEAD_DIM
    total = sums[0]
    for gi in range(1, len(POOL_WINDOWS)):
        total = jnp.where(lane_win == gi, sums[gi], total)
    return total * inv_cnt_ref[...] - pad_ref[0, pl.ds(m, ts), :]


def _pool_inverse_counts(seq):
    pos = jnp.arange(seq, dtype=jnp.int32)[:, None]
    win = jnp.repeat(jnp.asarray(POOL_WINDOWS, jnp.int32), HEAD_DIM)[None, :]
    cnt = jnp.minimum(pos + win - win // 2, seq) - jnp.maximum(pos - win // 2, 0)
    return 1.0 / cnt.astype(F32)


def _pool_blockdiag(w_pool):
    n_g = w_pool.shape[0]
    eye = jnp.eye(n_g, dtype=w_pool.dtype)
    return jnp.einsum('gcd,gh->gchd', w_pool, eye).reshape(n_g * HEAD_DIM, n_g * HEAD_DIM)


def _hgrn_constants():
    c = HGRN_CHUNK
    t = np.arange(c)
    s_col = np.tile(t, HEADS_PER_TILE)
    cum, masks = [], []
    for direction in range(2):
        tri = (t[None, :] <= t[:, None]) if direction == 0 else (t[None, :] >= t[:, None])
        tri = tri.astype(np.float32)
        mats, qms, bms = [tri], [], []
        for level in range(HGRN_LEVELS):
            half = HGRN_SUB << level
            start = (t // (2 * half)) * (2 * half)
            boundary = start + (half - 1 if direction == 0 else half)
            mats.append(tri[boundary])
            q_right = (t // half) % 2 == 1
            qm = q_right if direction == 0 else ~q_right
            qms.append(np.broadcast_to(qm[:, None], (c, LANES)))
            bms.append(t[:, None] // (2 * half) == s_col[None, :] // (2 * half))
        mats.append(tri[(t // HGRN_SUB) * HGRN_SUB + (0 if direction == 0 else HGRN_SUB - 1)])
        same_sub = t[:, None] // HGRN_SUB == s_col[None, :] // HGRN_SUB
        causal = (s_col[None, :] <= t[:, None]) if direction == 0 else (s_col[None, :] >= t[:, None])
        cum.append(np.tile(np.concatenate(mats, axis=0), (1, 3)))
        masks.append(np.stack(qms + bms[:HGRN_LEVELS - 1] + [same_sub & causal]).astype(np.float32))
    lane_head = np.arange(LANES) // HEAD_DIM
    sel = np.concatenate([(lane_head[:, None] == lane_head[None, :]) & (s_col[None, :] % HGRN_SUB == j)
                          for j in range(HGRN_SUB)], axis=0).astype(np.float32)
    return (jnp.asarray(np.stack(cum), BF16), jnp.asarray(np.stack(masks), F32), jnp.asarray(sel, BF16))


def _split3(x):
    x1 = x.astype(BF16)
    r1 = x - x1.astype(F32)
    x2 = r1.astype(BF16)
    x3 = (r1 - x2.astype(F32)).astype(BF16)
    return x1, x2, x3


def _by_head(x):
    lo = lax.broadcasted_iota(jnp.int32, x.shape, 1) < HEAD_DIM
    return jnp.concatenate([jnp.where(lo, x, 0.0), jnp.where(lo, 0.0, x)], axis=0).astype(BF16)


def _hgrn_gates(st, slot, cum_ref, ball_scr, k_scr):
    fl = st["f_ref"][st["at"]]
    lb = st["lb_ref"][:, st["cols"]]
    e = jnp.exp(-jnp.abs(fl))
    one_plus_e = 1.0 + e
    log_sig = jnp.minimum(fl, 0.0) - jnp.log(one_plus_e)
    rcp = 1.0 / one_plus_e
    k_scr[slot, st["stream"]] = (1.0 - lb) * jnp.where(fl >= 0, e * rcp, rcp)
    log_lb = jnp.log(lb)
    cc = jnp.log(1.0 - lb) + log_sig
    g = jnp.maximum(log_lb, cc) + jnp.log(1.0 + jnp.exp(-jnp.abs(log_lb - cc)))
    cum = cum_ref[st["direction"]]
    ball_scr[slot, st["stream"]] = jnp.dot(cum, jnp.concatenate(_split3(g * LOG2_E), axis=0),
                                           preferred_element_type=F32)


def _hgrn_products(st, slot, mask_ref, sel_ref, state_ref, ball_scr, k_scr, out_scr, *, pairwise):
    c = HGRN_CHUNK
    direction, stream = st["direction"], st["stream"]
    q, v = st["q_ref"][st["at"]], st["v_ref"][st["at"]]
    kk = k_scr[slot, stream]
    ball = ball_scr[slot, stream]
    b = ball[0:c]
    b_edge = b[c - 1:c] if direction == 0 else b[0:1]

    state = state_ref[stream]
    out_scr[slot, stream, 0] = _nt((q * jnp.exp2(b)).astype(BF16), state.astype(BF16))
    k_out = kk * jnp.exp2(b_edge - b)
    upd = _tn(v.astype(BF16), k_out.astype(BF16))
    r128 = lax.broadcasted_iota(jnp.int32, (LANES, LANES), 0) // HEAD_DIM
    c128 = lax.broadcasted_iota(jnp.int32, (LANES, LANES), 1) // HEAD_DIM
    state_ref[stream] = jnp.where(r128 == c128, state * jnp.exp2(b_edge) + upd, 0.0)

    att = None
    for level in range(HGRN_LEVELS):
        beta = ball[(level + 1) * c:(level + 2) * c]
        qm = mask_ref[direction, level]
        decay = jnp.exp2(-jnp.abs(b - beta))
        ql = q * decay * qm
        kl = kk * decay * (1.0 - qm)
        a = _nt(ql.astype(BF16), _by_head(kl))
        if level < HGRN_LEVELS - 1:
            a = a * mask_ref[direction, HGRN_LEVELS + level]
        att = a if att is None else att + a

    if pairwise:
        pair_cols = []
        for j in range(HGRN_SUB):
            pieces = []
            for u in range(c // HGRN_SUB):
                r = u * HGRN_SUB
                b_row = ball_scr[slot, stream, pl.ds(r + j, 1), :]
                k_row = k_scr[slot, stream, pl.ds(r + j, 1), :]
                pieces.append((q[r:r + HGRN_SUB] * k_row)
                              * jnp.exp2(jnp.minimum(b[r:r + HGRN_SUB] - b_row, 0.0)))
            pair_cols.append(jnp.concatenate(pieces, axis=0).astype(BF16))
        diag = jnp.dot(jnp.concatenate(pair_cols, axis=1), sel_ref[...], preferred_element_type=F32)
    else:
        ref = ball[(HGRN_LEVELS + 1) * c:(HGRN_LEVELS + 2) * c]
        diag = _nt((q * jnp.exp2(b - ref)).astype(BF16), _by_head(kk * jnp.exp2(ref - b)))
    out_scr[slot, stream, 1] = att + diag * mask_ref[direction, 2 * HGRN_LEVELS - 1]


def _hgrn_finish(st, slot, out_scr):
    v = st["v_ref"][st["at"]]
    st["o_ref"][st["at"]] = out_scr[slot, st["stream"], 0] + jnp.dot(
        out_scr[slot, st["stream"], 1].astype(BF16), _by_head(v), preferred_element_type=F32)


def _hgrn_body(qf_ref, vf_ref, ff_ref, qb_ref, vb_ref, fb_ref, lbf_ref, lbb_ref,
               cum_ref, mask_ref, sel_ref, of_ref, ob_ref, state_ref, ball_scr, k_scr, out_scr):
    @pl.when(pl.program_id(1) == 0)
    def _():
        state_ref[...] = jnp.zeros_like(state_ref)

    n_batch, n_rows = qf_ref.shape[0], qf_ref.shape[1]
    n_chunks = n_rows // HGRN_CHUNK
    n_tiles = HGRN_WIDTH // LANES
    per_direction = ((qf_ref, vf_ref, ff_ref, lbf_ref, of_ref), (qb_ref, vb_ref, fb_ref, lbb_ref, ob_ref))

    def streams_of(i):
        streams = []
        for seq_i in range(n_batch):
            for direction, (q_ref, v_ref, f_ref, lb_ref, o_ref) in enumerate(per_direction):
                chunk = i if direction == 0 else n_chunks - 1 - i
                rows = pl.ds(pl.multiple_of(chunk * HGRN_CHUNK, HGRN_CHUNK), HGRN_CHUNK)
                for tile in range(n_tiles):
                    cols = slice(tile * LANES, (tile + 1) * LANES)
                    streams.append(dict(direction=direction, cols=cols, at=(seq_i, rows, cols),
                                        stream=(seq_i * 2 + direction) * n_tiles + tile, q_ref=q_ref,
                                        v_ref=v_ref, f_ref=f_ref, lb_ref=lb_ref, o_ref=o_ref))
        return streams

    for st in streams_of(0):
        _hgrn_gates(st, 0, cum_ref, ball_scr, k_scr)
    out_scr[1] = jnp.zeros(out_scr.shape[1:], out_scr.dtype)

    def step_group(group, carry, *, pairwise):
        for k in range(HGRN_UNROLL):
            slot = k % 2
            i = HGRN_UNROLL * group + k
            for st in streams_of(jnp.maximum(i - 1, 0)):
                _hgrn_finish(st, 1 - slot, out_scr)
            for st in streams_of(jnp.minimum(i + 1, n_chunks - 1)):
                _hgrn_gates(st, 1 - slot, cum_ref, ball_scr, k_scr)
            for st in streams_of(i):
                _hgrn_products(st, slot, mask_ref, sel_ref, state_ref, ball_scr, k_scr, out_scr,
                               pairwise=pairwise)
        return carry

    factorisable = jnp.maximum(jnp.max(-ff_ref[...]), jnp.max(-fb_ref[...])) <= HGRN_SAFE_GATE
    for pairwise in (False, True):
        @pl.when(factorisable != pairwise)
        def _(pairwise=pairwise):
            lax.fori_loop(0, n_chunks // HGRN_UNROLL, functools.partial(step_group, pairwise=pairwise), 0)

    for st in streams_of(n_chunks - 1):
        _hgrn_finish(st, (n_chunks - 1) % 2, out_scr)


def hgrn_scan(rest, lb_fwd, lb_bwd, *, ts=1024):
    bsz, seq, _ = rest.shape
    ts = min(ts, seq)
    nt = seq // ts
    n_batch = HGRN_BATCH if bsz % HGRN_BATCH == 0 else 1
    n_streams = n_batch * 2 * HGRN_WIDTH // LANES
    cum, masks, sel = _hgrn_constants()

    def fwd(c0):
        return pl.BlockSpec((n_batch, ts, HGRN_WIDTH), lambda b, j: (b, j, c0))

    def bwd(c0):
        return pl.BlockSpec((n_batch, ts, HGRN_WIDTH), lambda b, j: (b, nt - 1 - j, c0))

    return pl.pallas_call(
        _hgrn_body,
        grid=(bsz // n_batch, nt),
        in_specs=[fwd(1), fwd(2), fwd(3), bwd(1), bwd(2), bwd(4),
                  _resident((1, HGRN_WIDTH)), _resident((1, HGRN_WIDTH)),
                  _resident(cum.shape), _resident(masks.shape), _resident(sel.shape)],
        out_specs=[fwd(0), bwd(0)],
        out_shape=[jax.ShapeDtypeStruct((bsz, seq, HGRN_WIDTH), F32)] * 2,
        scratch_shapes=[pltpu.VMEM((n_streams, LANES, LANES), F32),
                        pltpu.VMEM((2, n_streams) + cum.shape[1:2] + (LANES,), F32),
                        pltpu.VMEM((2, n_streams, HGRN_CHUNK, LANES), F32),
                        pltpu.VMEM((2, n_streams, 2, HGRN_CHUNK, LANES), F32)],
        compiler_params=_cparams(("parallel", "arbitrary")),
    )(rest, rest, rest, rest, rest, rest, lb_fwd.reshape(1, HGRN_WIDTH), lb_bwd.reshape(1, HGRN_WIDTH),
      cum, masks, sel)


def _post_body(x_ref, att_ref, u_ref, prev_ref, next_ref, inv_cnt_ref, of_ref, ob_ref, gate_ref,
               pw_ref, ps_ref, gain_ref, bd_ref, w_ref, g2_ref, wg_ref, wu_ref, wd_ref,
               o_ref, pad_ref, a_ref):
    tile, n_tiles = pl.program_id(1), pl.num_programs(1)
    centred = _pool_centred(tile, n_tiles, u_ref, prev_ref, next_ref, inv_cnt_ref, pad_ref)
    pooled = jnp.dot(centred.astype(BF16), pw_ref[...], preferred_element_type=F32) * ps_ref[...]
    o = of_ref[...] + ob_ref[...]
    y = o * lax.rsqrt(_head_mean_sq(o, bd_ref[...]) + EPS) * gain_ref[...]
    gate = gate_ref[...]
    rec = (y * (gate * jax.nn.sigmoid(gate))).astype(BF16)
    mixed = jnp.concatenate([att_ref[...].astype(BF16), pooled.astype(BF16), rec], axis=1)
    x = x_ref[...] + jnp.dot(mixed, w_ref[...], preferred_element_type=F32)
    o_ref[...] = _half_ffn(x, g2_ref, wg_ref, wu_ref, wd_ref, a_ref)


def post_mixer(x3, y_att, rest, o_fwd, o_bwd, pool_w, pool_scale, out_gain, w_out,
               ffn_gain, wg, wu, wd, *, tm=512):
    bsz, seq, d = x3.shape
    dff = wg.shape[1]
    tm = min(tm, seq)
    gate_tile = (rest.shape[2] - HGRN_WIDTH) // HGRN_WIDTH
    per_tile = tm // POOL_HALO
    last_halo = seq // POOL_HALO - 1

    def rowblock(width, col=0):
        return pl.BlockSpec((None, tm, width), lambda b, t: (b, t, col))

    return pl.pallas_call(
        _post_body,
        grid=(bsz, seq // tm),
        in_specs=[rowblock(d), rowblock(ATT_WIDTH), rowblock(POOL_WIDTH),
                  pl.BlockSpec((None, POOL_HALO, POOL_WIDTH),
                               lambda b, t: (b, jnp.maximum(t * per_tile - 1, 0), 0)),
                  pl.BlockSpec((None, POOL_HALO, POOL_WIDTH),
                               lambda b, t: (b, jnp.minimum((t + 1) * per_tile, last_halo), 0)),
                  pl.BlockSpec((tm, POOL_WIDTH), lambda b, t: (t, 0)),
                  rowblock(HGRN_WIDTH), rowblock(HGRN_WIDTH), rowblock(HGRN_WIDTH, gate_tile),
                  _resident((POOL_WIDTH, POOL_WIDTH)), _resident((1, POOL_WIDTH)),
                  _resident((1, HGRN_WIDTH)), _resident((HGRN_WIDTH, HGRN_WIDTH)), _resident(w_out.shape),
                  _resident((1, d)), _resident((d, dff)), _resident((d, dff)), _resident((dff, d))],
        out_specs=rowblock(d),
        out_shape=jax.ShapeDtypeStruct((bsz, seq, d), F32),
        scratch_shapes=[pltpu.VMEM((len(POOL_WINDOWS), tm + 2 * POOL_MARGIN, POOL_WIDTH), F32),
                        pltpu.VMEM((tm, dff), BF16)],
        compiler_params=_cparams(("parallel", "parallel")),
    )(x3, y_att, rest, rest, rest, _pool_inverse_counts(seq), o_fwd, o_bwd, rest,
      _pool_blockdiag(pool_w).astype(BF16), pool_scale.reshape(1, POOL_WIDTH),
      jnp.tile(out_gain, HGRN_WIDTH // HEAD_DIM).reshape(1, HGRN_WIDTH), _head_blockdiag(HGRN_WIDTH),
      w_out, ffn_gain.reshape(1, d), wg, wu, wd)


def kernel(x, ffn1_norm, ffn1_w_gate, ffn1_w_up, ffn1_w_down, mix_norm, w_in, q_norm, k_norm, rel_bias,
           pool_w, pool_scale, hgrn_lb_logits, hgrn_norm, w_out, ffn2_norm, ffn2_w_gate, ffn2_w_up,
           ffn2_w_down):
    bsz, seq, d = x.shape
    depth = w_in.shape[0]
    n = bsz * seq
    h = x.astype(F32).reshape(n, d)
    bias = attention_bias(rel_bias)
    lb_cum = jnp.cumsum(jax.nn.softmax(hgrn_lb_logits.astype(F32), axis=1), axis=1)
    lb_all = lb_cum - lb_cum[:, :1]
    for l in range(depth):
        h, qkv, rest = pre_mixer(h, seq, ffn1_norm[l], ffn1_w_gate[l].astype(BF16),
                                 ffn1_w_up[l].astype(BF16), ffn1_w_down[l].astype(BF16),
                                 mix_norm[l], w_in[l].astype(BF16), q_norm[l], k_norm[l])
        rest = rest.reshape(bsz, seq, -1)
        y_att = attention(qkv, bias)
        o_fwd, o_bwd = hgrn_scan(rest, lb_all[0, l], lb_all[1, l])
        h = post_mixer(h.reshape(bsz, seq, d), y_att, rest, o_fwd, o_bwd, pool_w[l], pool_scale[l],
                       hgrn_norm[l], w_out[l].astype(BF16), ffn2_norm[l], ffn2_w_gate[l].astype(BF16),
                       ffn2_w_up[l].astype(BF16), ffn2_w_down[l].astype(BF16)).reshape(n, d)
    return h.reshape(bsz, seq, d).astype(x.dtype)
```
